```python
import math
import jax, jax.numpy as jnp
from jax import lax
import numpy as np

D_MODEL = 1024
BATCH = 8
SEQ = 2048
DEPTH = 2
DEC_BATCH = 128
DEC_SEQ = 8
PAST_LEN = 16384
PAGE_SIZE = 128

EPS = 1e-6
N_EVEN = (DEPTH + 1) // 2
N_ODD = DEPTH // 2
D_POOL = D_MODEL // 2
POOL_WINDOWS = (2, 4, 8, 16)
POOL_GROUPS = len(POOL_WINDOWS)
POOL_GC = D_POOL // POOL_GROUPS
POOL_PREV = max(POOL_WINDOWS) - 1
D_CONV = D_MODEL // 2
CONV_W = 3
D_AB_IN = D_POOL + 3 * D_CONV
D_SGU = D_MODEL
SGU_HEADS = 8
SGU_HD = D_SGU // SGU_HEADS
CHUNK = 128
D_FF = 4 * D_MODEL

kernel_name = "hybrid_pool_conv_sgu_decoder_step"


def rmsnorm(x, g):
    xf = x.astype(jnp.float32)
    y = xf * lax.rsqrt(jnp.mean(xf * xf, axis=-1, keepdims=True) + EPS) * g.astype(jnp.float32)
    return y.astype(x.dtype)


def pool_mixer(u, prev, start_pos, w_grp, scale):
    b, s, _ = u.shape
    full = jnp.concatenate([prev.astype(u.dtype), u], axis=1)
    cs = jnp.cumsum(full.astype(jnp.float32), axis=1)
    cs = jnp.concatenate([jnp.zeros_like(cs[:, :1]), cs], axis=1)
    hi = cs[:, POOL_PREV + 1:]
    pos = (jnp.arange(s) + start_pos).astype(jnp.float32)
    outs = []
    for g, w in enumerate(POOL_WINDOWS):
        sl = slice(g * POOL_GC, (g + 1) * POOL_GC)
        lo = cs[:, POOL_PREV + 1 - w:POOL_PREV + 1 - w + s, sl]
        cnt = jnp.minimum(pos + 1.0, float(w))[None, :, None]
        outs.append((hi[..., sl] - lo) / cnt - u[..., sl].astype(jnp.float32))
    d = jnp.stack(outs, axis=2).astype(u.dtype)
    y = jnp.einsum('bsgc,gcd->bsgd', d, w_grp).reshape(b, s, D_POOL) * scale
    return y, full[:, -POOL_PREV:]


def short_conv(z, prev, w):
    s = z.shape[1]
    full = jnp.concatenate([prev.astype(z.dtype), z], axis=1)
    out = w[0] * full[:, :s]
    for k in range(1, CONV_W):
        out = out + w[k] * full[:, k:k + s]
    return out, full[:, -(CONV_W - 1):]


def pool_conv_mixer(h, pool_prev, conv_prev, start_pos, w_in, w_grp, scale, conv_w, w_out):
    p = h @ w_in
    u = p[..., :D_POOL]
    xb = p[..., D_POOL:D_POOL + D_CONV]
    gate_b = p[..., D_POOL + D_CONV:D_POOL + 2 * D_CONV]
    gate_c = p[..., D_POOL + 2 * D_CONV:]
    ya, new_pool = pool_mixer(u, pool_prev, start_pos, w_grp, scale)
    cz, new_conv = short_conv(gate_c * xb, conv_prev, conv_w)
    yb = gate_b * cz
    return jnp.concatenate([ya, yb], axis=-1) @ w_out, new_pool, new_conv


def chunk_spatial(v, w_s, b_s):
    b, s, h, hd = v.shape
    mask = jnp.tril(jnp.ones((CHUNK, CHUNK), dtype=w_s.dtype))
    wm = w_s * mask
    if s < CHUNK:
        out = jnp.einsum('hts,bshd->bthd', wm[:, :s, :s], v)
        return out + jnp.transpose(b_s[:, :s])[None, :, :, None]
    n = -(-s // CHUNK)
    vp = jnp.pad(v, ((0, 0), (0, n * CHUNK - s), (0, 0), (0, 0))).reshape(b, n, CHUNK, h, hd)
    out = jnp.einsum('hts,bnshd->bnthd', wm, vp) + jnp.transpose(b_s)[None, None, :, :, None]
    return out.reshape(b, n * CHUNK, h, hd)[:, :s]


def chunk_mlp_mixer(h, w_uv, g_v, w_s, b_s, w_out):
    b, s, _ = h.shape
    p = h @ w_uv
    u = p[..., :D_SGU]
    v = rmsnorm(p[..., D_SGU:], g_v)
    sv = chunk_spatial(v.reshape(b, s, SGU_HEADS, SGU_HD), w_s, b_s).reshape(b, s, D_SGU)
    return (u * sv) @ w_out, v


def trunk(x, pool_prev, conv_prev, start_pos,
          g_mix_pre, g_mix_post, g_ffn_pre, g_ffn_post,
          w_in_ab, w_pool_grp, pool_scale, conv_w, w_out_ab,
          w_uv, g_v, w_spatial, b_spatial, w_out_c, w_up, w_down):
    new_pool, new_conv, new_v = [], [], []
    for l in range(DEPTH):
        h = rmsnorm(x, g_mix_pre[l])
        if l % 2 == 0:
            i = l // 2
            m, npool, nconv = pool_conv_mixer(h, pool_prev[i], conv_prev[i], start_pos,
                                              w_in_ab[i], w_pool_grp[i], pool_scale[i],
                                              conv_w[i], w_out_ab[i])
            new_pool.append(npool)
            new_conv.append(nconv)
        else:
            i = l // 2
            m, v = chunk_mlp_mixer(h, w_uv[i], g_v[i], w_spatial[i], b_spatial[i], w_out_c[i])
            new_v.append(v)
        x = x + rmsnorm(m, g_mix_post[l])
        h = rmsnorm(x, g_ffn_pre[l])
        f = jnp.square(jax.nn.relu(h @ w_up[l])) @ w_down[l]
        x = x + rmsnorm(f, g_ffn_post[l])
    return x, jnp.stack(new_pool), jnp.stack(new_conv), jnp.stack(new_v)


def setup_inputs(seed: int = 0) -> dict:
    key = jax.random.key(seed)
    ks = jax.random.split(key, 24)
    nrm = lambda k, shape, sc: jax.random.normal(k, shape, jnp.float32) * sc
    gain = lambda k, shape: 1.0 + 0.05 * jax.random.normal(k, shape, jnp.float32)
    return {
        "x_prompt": nrm(ks[0], (BATCH, SEQ, D_MODEL), 1.0),
        "x_sample": nrm(ks[1], (DEC_BATCH, DEC_SEQ, D_MODEL), 1.0),
        "state_pool": nrm(ks[2], (N_EVEN, DEC_BATCH, POOL_PREV, D_POOL), 1.0),
        "state_conv": nrm(ks[3], (N_EVEN, DEC_BATCH, CONV_W - 1, D_CONV), 1.0),
        "g_mix_pre": gain(ks[4], (DEPTH, D_MODEL)),
        "g_mix_post": gain(ks[5], (DEPTH, D_MODEL)),
        "g_ffn_pre": gain(ks[6], (DEPTH, D_MODEL)),
        "g_ffn_post": gain(ks[7], (DEPTH, D_MODEL)),
        "w_in_ab": nrm(ks[8], (N_EVEN, D_MODEL, D_AB_IN), D_MODEL ** -0.5),
        "w_pool_grp": nrm(ks[9], (N_EVEN, POOL_GROUPS, POOL_GC, POOL_GC), POOL_GC ** -0.5),
        "pool_scale": gain(ks[10], (N_EVEN, D_POOL)),
        "conv_w": nrm(ks[11], (N_EVEN, CONV_W, D_CONV), CONV_W ** -0.5),
        "w_out_ab": nrm(ks[12], (N_EVEN, D_POOL + D_CONV, D_MODEL), (D_POOL + D_CONV) ** -0.5),
        "w_uv": nrm(ks[13], (N_ODD, D_MODEL, 2 * D_SGU), D_MODEL ** -0.5),
        "g_v": gain(ks[14], (N_ODD, D_SGU)),
        "w_spatial": nrm(ks[15], (N_ODD, SGU_HEADS, CHUNK, CHUNK), CHUNK ** -0.5),
        "b_spatial": gain(ks[16], (N_ODD, SGU_HEADS, CHUNK)),
        "w_out_c": nrm(ks[17], (N_ODD, D_SGU, D_MODEL), D_SGU ** -0.5),
        "w_up": nrm(ks[18], (DEPTH, D_MODEL, D_FF), D_MODEL ** -0.5),
        "w_down": nrm(ks[19], (DEPTH, D_FF, D_MODEL), D_FF ** -0.5),
    }


def reference(x_prompt, x_sample, state_pool, state_conv,
              g_mix_pre, g_mix_post, g_ffn_pre, g_ffn_post,
              w_in_ab, w_pool_grp, pool_scale, conv_w, w_out_ab,
              w_uv, g_v, w_spatial, b_spatial, w_out_c, w_up, w_down):
    b = x_prompt.shape[0]
    zero_pool = jnp.zeros((N_EVEN, b, POOL_PREV, D_POOL), x_prompt.dtype)
    zero_conv = jnp.zeros((N_EVEN, b, CONV_W - 1, D_CONV), x_prompt.dtype)
    y_prompt, pool_p, conv_p, _ = trunk(
        x_prompt, zero_pool, zero_conv, 0,
        g_mix_pre, g_mix_post, g_ffn_pre, g_ffn_post,
        w_in_ab, w_pool_grp, pool_scale, conv_w, w_out_ab,
        w_uv, g_v, w_spatial, b_spatial, w_out_c, w_up, w_down)
    y_sample, pool_s, conv_s, v_s = trunk(
        x_sample, state_pool, state_conv, PAST_LEN,
        g_mix_pre, g_mix_post, g_ffn_pre, g_ffn_post,
        w_in_ab, w_pool_grp, pool_scale, conv_w, w_out_ab,
        w_uv, g_v, w_spatial, b_spatial, w_out_c, w_up, w_down)
    return (y_prompt, y_sample, pool_p, pool_s, conv_p, conv_s, v_s)
```

```python
import functools

import jax
import jax.numpy as jnp
from jax import lax
from jax.experimental import pallas as pl
from jax.experimental.pallas import tpu as pltpu

D_MODEL = 1024
EPS = 1e-6
D_POOL = D_MODEL // 2
POOL_WINDOWS = (2, 4, 8, 16)
POOL_GC = D_POOL // len(POOL_WINDOWS)
POOL_PREV = max(POOL_WINDOWS) - 1
D_CONV = D_MODEL // 2
CONV_W = 3
D_AB_IN = D_POOL + 3 * D_CONV
D_SGU = D_MODEL
SGU_HEADS = 8
SGU_HD = D_SGU // SGU_HEADS
CHUNK = 128
D_FF = 4 * D_MODEL
PAST_LEN = 16384

HALO = 16
TILE_M = 512
VMEM_LIMIT_BYTES = 60 * 1024 * 1024

F32 = jnp.float32
BF16 = jnp.bfloat16


def _rms(x, g):
    return x * lax.rsqrt(jnp.mean(x * x, axis=-1, keepdims=True) + EPS) * g


def _dot(a, b):
    return jnp.dot(a, b, preferred_element_type=F32)


def _shift_rows(a, k):
    return pltpu.roll(a, k, axis=0)


def _window_sums(full_u):
    sums = []
    for g, w in enumerate(POOL_WINDOWS):
        s = full_u[:, g * POOL_GC:(g + 1) * POOL_GC]
        k = 1
        while k < w:
            s = s + _shift_rows(s, k)
            k *= 2
        sums.append(s)
    return sums


def _pool_out(sums, u, pos, w_grp, scale):
    outs = []
    for g, w in enumerate(POOL_WINDOWS):
        inv_cnt = 1.0 / jnp.minimum(pos + 1.0, float(w))
        outs.append(sums[g] * inv_cnt - u[:, g * POOL_GC:(g + 1) * POOL_GC])
    d = jnp.concatenate(outs, axis=1).astype(BF16)
    return _dot(d, w_grp) * scale


def _conv_rows(full_cx, conv_w):
    return (conv_w[0:1] * _shift_rows(full_cx, 2) + conv_w[1:2] * _shift_rows(full_cx, 1)
            + conv_w[2:3] * full_cx)


def _ffn(x, g_pre, g_post, w_up_ref, w_down_ref):
    h = _rms(x, g_pre).astype(BF16)
    a = jnp.maximum(_dot(h, w_up_ref[...]), 0.0)
    a = (a * a).astype(BF16)
    f = _dot(a, w_down_ref[...])
    return x + _rms(f, g_post)


def _layer0_prompt_kernel(x_ref, g_ref, w_in_ref, w_grp_ref, scale_ref, conv_w_ref, w_out_ref,
                          w_up_ref, w_down_ref,
                          y_ref, u_tail_ref, cx_tail_ref,
                          halo_u, halo_cx):
    tm = x_ref.shape[0]
    t = pl.program_id(1)

    @pl.when(t == 0)
    def _():
        halo_u[...] = jnp.zeros_like(halo_u)
        halo_cx[...] = jnp.zeros_like(halo_cx)

    x = x_ref[...]
    g = g_ref[...]
    h = _rms(x, g[0:1]).astype(BF16)
    p = _dot(h, w_in_ref[...])
    u = p[:, :D_POOL]
    cx = p[:, D_POOL + 2 * D_CONV:] * p[:, D_POOL:D_POOL + D_CONV]
    gate_b = p[:, D_POOL + D_CONV:D_POOL + 2 * D_CONV]

    full_u = jnp.concatenate([halo_u[...], u], axis=0)
    full_cx = jnp.concatenate([halo_cx[...], cx], axis=0)
    halo_u[...] = u[tm - HALO:]
    halo_cx[...] = cx[tm - HALO:]
    u_tail_ref[...] = u[tm - HALO:]
    cx_tail_ref[...] = cx[tm - HALO:]

    pos = (lax.broadcasted_iota(jnp.int32, (tm, 1), 0) + t * tm).astype(F32)
    sums = [s[HALO:] for s in _window_sums(full_u)]
    ya = _pool_out(sums, u, pos, w_grp_ref[...], scale_ref[...])
    cz = _conv_rows(full_cx, conv_w_ref[...])[HALO:]
    yab = jnp.concatenate([ya, gate_b * cz], axis=1).astype(BF16)
    x = x + _rms(_dot(yab, w_out_ref[...]), g[1:2])
    y_ref[...] = _ffn(x, g[2:3], g[3:4], w_up_ref, w_down_ref)


def _layer0_sample_kernel(x_ref, pool_ref, conv_ref, g_ref, w_in_ref, w_grp_ref, scale_ref,
                          conv_w_ref, w_out_ref, w_up_ref, w_down_ref,
                          y_ref, u_ref, cx_ref, *, seq):
    tm = x_ref.shape[0]
    tb = tm // seq
    x = x_ref[...]
    g = g_ref[...]
    h = _rms(x, g[0:1]).astype(BF16)
    p = _dot(h, w_in_ref[...])
    u = p[:, :D_POOL]
    cx = p[:, D_POOL + 2 * D_CONV:] * p[:, D_POOL:D_POOL + D_CONV]
    gate_b = p[:, D_POOL + D_CONV:D_POOL + 2 * D_CONV]
    u_ref[...] = u
    cx_ref[...] = cx

    grp = HALO + seq
    full_u = jnp.concatenate([pool_ref[...], u.reshape(tb, seq, D_POOL)], axis=1)
    full_cx = jnp.concatenate([conv_ref[...], cx.reshape(tb, seq, D_CONV)], axis=1)
    full_u = full_u.reshape(tb * grp, D_POOL)
    full_cx = full_cx.reshape(tb * grp, D_CONV)

    def new_rows(a):
        return a.reshape(tb, grp, a.shape[-1])[:, HALO:].reshape(tm, a.shape[-1])

    pos = (lax.broadcasted_iota(jnp.int32, (tb, seq, 1), 1).reshape(tm, 1) + PAST_LEN).astype(F32)
    sums = [new_rows(s) for s in _window_sums(full_u)]
    ya = _pool_out(sums, u, pos, w_grp_ref[...], scale_ref[...])
    cz = new_rows(_conv_rows(full_cx, conv_w_ref[...]))
    yab = jnp.concatenate([ya, gate_b * cz], axis=1).astype(BF16)
    x = x + _rms(_dot(yab, w_out_ref[...]), g[1:2])
    y_ref[...] = _ffn(x, g[2:3], g[3:4], w_up_ref, w_down_ref)


def _layer1_kernel(x_ref, g_ref, w_uv_ref, g_v_ref, w_sp_ref, b_sp_ref, w_out_ref,
                   w_up_ref, w_down_ref, y_ref, *maybe_v_ref):
    tm = x_ref.shape[0]
    x = x_ref[...]
    g = g_ref[...]
    h = _rms(x, g[0:1]).astype(BF16)
    p = _dot(h, w_uv_ref[...])
    u = p[:, :D_SGU]
    v = _rms(p[:, D_SGU:], g_v_ref[...])
    if maybe_v_ref:
        maybe_v_ref[0][...] = v
    vb = v.astype(BF16)

    r = lax.broadcasted_iota(jnp.int32, (CHUNK, CHUNK), 0)
    c = lax.broadcasted_iota(jnp.int32, (CHUNK, CHUNK), 1)
    mask = (r >= c).astype(F32)
    bias = b_sp_ref[...]
    cols = []
    for hh in range(SGU_HEADS):
        wm = (w_sp_ref[hh] * mask).astype(BF16)
        lanes = slice(hh * SGU_HD, (hh + 1) * SGU_HD)
        blocks = [_dot(wm, vb[ci * CHUNK:(ci + 1) * CHUNK, lanes]) + bias[:, lanes]
                  for ci in range(tm // CHUNK)]
        cols.append(jnp.concatenate(blocks, axis=0))
    sv = jnp.concatenate(cols, axis=1)
    x = x + _rms(_dot((u * sv).astype(BF16), w_out_ref[...]), g[1:2])
    y_ref[...] = _ffn(x, g[2:3], g[3:4], w_up_ref, w_down_ref)


def _resident(shape):
    nd = len(shape)
    return pl.BlockSpec(shape, lambda *_: (0,) * nd, pipeline_mode=pl.Buffered(1))


def _params(n_axes):
    return pltpu.CompilerParams(dimension_semantics=("arbitrary",) * n_axes,
                                vmem_limit_bytes=VMEM_LIMIT_BYTES)


def _layer0_weights(gains, w_in, w_grp_bd, scale, conv_w, w_out, w_up, w_down):
    arrs = [gains, w_in, w_grp_bd, scale, conv_w, w_out, w_up, w_down]
    return arrs, [_resident(a.shape) for a in arrs]


def _layer0_prompt(x, weights):
    b, s, d = x.shape
    tm = TILE_M
    arrs, specs = _layer0_weights(*weights)
    tile = pl.BlockSpec((None, tm, d), lambda i, j: (i, j, 0))
    tail = pl.BlockSpec((None, HALO, D_POOL), lambda i, j: (i, 0, 0))
    return pl.pallas_call(
        _layer0_prompt_kernel,
        grid=(b, s // tm),
        in_specs=[tile] + specs,
        out_specs=[tile, tail, tail],
        out_shape=[jax.ShapeDtypeStruct(x.shape, F32),
                   jax.ShapeDtypeStruct((b, HALO, D_POOL), F32),
                   jax.ShapeDtypeStruct((b, HALO, D_CONV), F32)],
        scratch_shapes=[pltpu.VMEM((HALO, D_POOL), F32), pltpu.VMEM((HALO, D_CONV), F32)],
        compiler_params=_params(2),
        name="layer0_prompt",
    )(x, *arrs)


def _layer0_sample(x, pool_hist, conv_hist, weights, seq):
    n, d = x.shape
    tm = TILE_M
    tb = tm // seq
    arrs, specs = _layer0_weights(*weights)
    tile = pl.BlockSpec((tm, d), lambda i: (i, 0))
    half = pl.BlockSpec((tm, D_POOL), lambda i: (i, 0))
    hist = pl.BlockSpec((tb, HALO, D_POOL), lambda i: (i, 0, 0))
    return pl.pallas_call(
        functools.partial(_layer0_sample_kernel, seq=seq),
        grid=(n // tm,),
        in_specs=[tile, hist, hist] + specs,
        out_specs=[tile, half, half],
        out_shape=[jax.ShapeDtypeStruct(x.shape, F32),
                   jax.ShapeDtypeStruct((n, D_POOL), F32),
                   jax.ShapeDtypeStruct((n, D_CONV), F32)],
        compiler_params=_params(1),
        name="layer0_sample",
    )(x, pool_hist, conv_hist, *arrs)


def _layer1(x, weights, emit_v, name):
    n, d = x.shape
    tm = TILE_M
    arrs = list(weights)
    tile = pl.BlockSpec((tm, d), lambda i: (i, 0))
    out_specs = [tile, tile] if emit_v else [tile]
    out_shape = [jax.ShapeDtypeStruct(x.shape, F32)] * len(out_specs)
    return pl.pallas_call(
        _layer1_kernel,
        grid=(n // tm,),
        in_specs=[tile] + [_resident(a.shape) for a in arrs],
        out_specs=out_specs,
        out_shape=out_shape,
        compiler_params=_params(1),
        name=name,
    )(x, *arrs)


def _block_diag(w):
    g, c, _ = w.shape
    eye = jnp.eye(g, dtype=w.dtype)
    return (eye[:, None, :, None] * w[:, :, None, :]).reshape(g * c, g * c)


def kernel(x_prompt, x_sample, state_pool, state_conv, g_mix_pre, g_mix_post, g_ffn_pre,
           g_ffn_post, w_in_ab, w_pool_grp, pool_scale, conv_w, w_out_ab, w_uv, g_v, w_spatial,
           b_spatial, w_out_c, w_up, w_down):
    bp, sp, d = x_prompt.shape
    bs, ss, _ = x_sample.shape

    def gains(l):
        return jnp.stack([g_mix_pre[l], g_mix_post[l], g_ffn_pre[l], g_ffn_post[l]])

    w0 = (gains(0), w_in_ab[0].astype(BF16), _block_diag(w_pool_grp[0]).astype(BF16),
          pool_scale[0][None], conv_w[0], w_out_ab[0].astype(BF16), w_up[0].astype(BF16),
          w_down[0].astype(BF16))

    bias_p = jnp.repeat(b_spatial[0].T, SGU_HD, axis=1)
    reps = CHUNK // ss
    w_sp_s = jax.vmap(lambda w: jnp.kron(jnp.eye(reps, dtype=w.dtype), w))(
        w_spatial[0][:, :ss, :ss])
    bias_s = jnp.tile(bias_p[:ss], (reps, 1))
    w1_common = (gains(1), w_uv[0].astype(BF16), g_v[0][None])
    w1_tail = (w_out_c[0].astype(BF16), w_up[1].astype(BF16), w_down[1].astype(BF16))

    y0, u_tail, cx_tail = _layer0_prompt(x_prompt, w0)
    (y_prompt,) = _layer1(y0.reshape(bp * sp, d), w1_common + (w_spatial[0], bias_p) + w1_tail,
                          False, "layer1_prompt")
    y_prompt = y_prompt.reshape(bp, sp, d)
    pool_p = u_tail[:, HALO - POOL_PREV:][None]
    conv_p = cx_tail[:, HALO - (CONV_W - 1):][None]

    pool_hist = jnp.pad(state_pool[0], ((0, 0), (HALO - POOL_PREV, 0), (0, 0)))
    conv_hist = jnp.pad(state_conv[0], ((0, 0), (HALO - (CONV_W - 1), 0), (0, 0)))
    ys0, u_s, cx_s = _layer0_sample(x_sample.reshape(bs * ss, d), pool_hist, conv_hist, w0, ss)
    ys1, v_s = _layer1(ys0, w1_common + (w_sp_s, bias_s) + w1_tail, True, "layer1_sample")
    y_sample = ys1.reshape(bs, ss, d)
    u_s = u_s.reshape(bs, ss, D_POOL)
    cx_s = cx_s.reshape(bs, ss, D_CONV)
    pool_s = jnp.concatenate([state_pool[0][:, ss:], u_s], axis=1)[None]
    conv_s = cx_s[:, ss - (CONV_W - 1):][None]
    v_s = v_s.reshape(bs, ss, D_SGU)[None]
    return (y_prompt, y_sample, pool_p, pool_s, conv_p, conv_s, v_s)
```

```python
import functools

import jax
import jax.numpy as jnp
from jax import lax
from jax.experimental import pallas as pl
from jax.experimental.pallas import tpu as pltpu

D_MODEL = 1024
EPS = 1e-6
D_POOL = D_MODEL // 2
POOL_WINDOWS = (2, 4, 8, 16)
POOL_GC = D_POOL // len(POOL_WINDOWS)
POOL_PREV = max(POOL_WINDOWS) - 1
D_CONV = D_MODEL // 2
CONV_W = 3
D_AB_IN = D_POOL + 3 * D_CONV
D_SGU = D_MODEL
SGU_HEADS = 8
SGU_HD = D_SGU // SGU_HEADS
CHUNK = 128
D_FF = 4 * D_MODEL
PAST_LEN = 16384

HALO = 16
TILE_M_PROMPT = 1024
TILE_M_SAMPLE = 512
VMEM_LIMIT_BYTES = 60 * 1024 * 1024

F32 = jnp.float32
BF16 = jnp.bfloat16


def _rms(x, g):
    return x * lax.rsqrt(jnp.mean(x * x, axis=-1, keepdims=True) + EPS) * g


def _dot(a, b):
    return jnp.dot(a, b, preferred_element_type=F32)


def _shift_rows(a, k):
    return pltpu.roll(a, k, axis=0)


def _window_sums(full_u):
    sums = []
    for g, w in enumerate(POOL_WINDOWS):
        s = full_u[:, g * POOL_GC:(g + 1) * POOL_GC]
        k = 1
        while k < w:
            s = s + _shift_rows(s, k)
            k *= 2
        sums.append(s)
    return sums


def _pool_out(sums, u, pos, w_grp, scale):
    outs = []
    for g, w in enumerate(POOL_WINDOWS):
        inv_cnt = 1.0 / jnp.minimum(pos + 1.0, float(w))
        outs.append(sums[g] * inv_cnt - u[:, g * POOL_GC:(g + 1) * POOL_GC])
    d = jnp.concatenate(outs, axis=1).astype(BF16)
    return _dot(d, w_grp) * scale


def _conv_rows(full_cx, conv_w):
    return (conv_w[0:1] * _shift_rows(full_cx, 2) + conv_w[1:2] * _shift_rows(full_cx, 1)
            + conv_w[2:3] * full_cx)


def _ffn(x, g_pre, g_post, w_up_ref, w_down_ref):
    h = _rms(x, g_pre).astype(BF16)
    a = jnp.maximum(_dot(h, w_up_ref[...]), 0.0)
    a = (a * a).astype(BF16)
    f = _dot(a, w_down_ref[...])
    return x + _rms(f, g_post)


def _layer0_prompt_kernel(x_ref, g_ref, w_in_ref, w_grp_ref, scale_ref, conv_w_ref, w_out_ref,
                          w_up_ref, w_down_ref,
                          y_ref, u_tail_ref, cx_tail_ref,
                          halo_u, halo_cx):
    tm = x_ref.shape[0]
    t = pl.program_id(1)

    @pl.when(t == 0)
    def _():
        halo_u[...] = jnp.zeros_like(halo_u)
        halo_cx[...] = jnp.zeros_like(halo_cx)

    x = x_ref[...]
    g = g_ref[...]
    h = _rms(x, g[0:1]).astype(BF16)
    p = _dot(h, w_in_ref[...])
    u = p[:, :D_POOL]
    cx = p[:, D_POOL + 2 * D_CONV:] * p[:, D_POOL:D_POOL + D_CONV]
    gate_b = p[:, D_POOL + D_CONV:D_POOL + 2 * D_CONV]

    full_u = jnp.concatenate([halo_u[...], u], axis=0)
    full_cx = jnp.concatenate([halo_cx[...], cx], axis=0)
    halo_u[...] = u[tm - HALO:]
    halo_cx[...] = cx[tm - HALO:]
    u_tail_ref[...] = u[tm - HALO:]
    cx_tail_ref[...] = cx[tm - HALO:]

    pos = (lax.broadcasted_iota(jnp.int32, (tm, 1), 0) + t * tm).astype(F32)
    sums = [s[HALO:] for s in _window_sums(full_u)]
    ya = _pool_out(sums, u, pos, w_grp_ref[...], scale_ref[...])
    cz = _conv_rows(full_cx, conv_w_ref[...])[HALO:]
    yab = jnp.concatenate([ya, gate_b * cz], axis=1).astype(BF16)
    x = x + _rms(_dot(yab, w_out_ref[...]), g[1:2])
    y_ref[...] = _ffn(x, g[2:3], g[3:4], w_up_ref, w_down_ref)


def _layer0_sample_kernel(x_ref, pool_ref, conv_ref, g_ref, w_in_ref, w_grp_ref, scale_ref,
                          conv_w_ref, w_out_ref, w_up_ref, w_down_ref,
                          y_ref, u_ref, cx_ref, *, seq):
    tm = x_ref.shape[0]
    tb = tm // seq
    x = x_ref[...]
    g = g_ref[...]
    h = _rms(x, g[0:1]).astype(BF16)
    p = _dot(h, w_in_ref[...])
    u = p[:, :D_POOL]
    cx = p[:, D_POOL + 2 * D_CONV:] * p[:, D_POOL:D_POOL + D_CONV]
    gate_b = p[:, D_POOL + D_CONV:D_POOL + 2 * D_CONV]
    u_ref[...] = u
    cx_ref[...] = cx

    grp = HALO + seq
    full_u = jnp.concatenate([pool_ref[...], u.reshape(tb, seq, D_POOL)], axis=1)
    full_cx = jnp.concatenate([conv_ref[...], cx.reshape(tb, seq, D_CONV)], axis=1)
    full_u = full_u.reshape(tb * grp, D_POOL)
    full_cx = full_cx.reshape(tb * grp, D_CONV)

    def new_rows(a):
        return a.reshape(tb, grp, a.shape[-1])[:, HALO:].reshape(tm, a.shape[-1])

    pos = (lax.broadcasted_iota(jnp.int32, (tb, seq, 1), 1).reshape(tm, 1) + PAST_LEN).astype(F32)
    sums = [new_rows(s) for s in _window_sums(full_u)]
    ya = _pool_out(sums, u, pos, w_grp_ref[...], scale_ref[...])
    cz = new_rows(_conv_rows(full_cx, conv_w_ref[...]))
    yab = jnp.concatenate([ya, gate_b * cz], axis=1).astype(BF16)
    x = x + _rms(_dot(yab, w_out_ref[...]), g[1:2])
    y_ref[...] = _ffn(x, g[2:3], g[3:4], w_up_ref, w_down_ref)


def _layer1_kernel(x_ref, g_ref, w_uv_ref, g_v_ref, w_sp_ref, b_sp_ref, w_out_ref,
                   w_up_ref, w_down_ref, y_ref, *maybe_v_ref):
    tm = x_ref.shape[0]
    x = x_ref[...]
    g = g_ref[...]
    h = _rms(x, g[0:1]).astype(BF16)
    p = _dot(h, w_uv_ref[...])
    u = p[:, :D_SGU]
    v = _rms(p[:, D_SGU:], g_v_ref[...])
    if maybe_v_ref:
        maybe_v_ref[0][...] = v
    vb = v.astype(BF16)

    r = lax.broadcasted_iota(jnp.int32, (CHUNK, CHUNK), 0)
    c = lax.broadcasted_iota(jnp.int32, (CHUNK, CHUNK), 1)
    mask = (r >= c).astype(F32)
    bias = b_sp_ref[...]
    cols = []
    for hh in range(SGU_HEADS):
        wm = (w_sp_ref[hh] * mask).astype(BF16)
        lanes = slice(hh * SGU_HD, (hh + 1) * SGU_HD)
        blocks = [_dot(wm, vb[ci * CHUNK:(ci + 1) * CHUNK, lanes]) + bias[:, lanes]
                  for ci in range(tm // CHUNK)]
        cols.append(jnp.concatenate(blocks, axis=0))
    sv = jnp.concatenate(cols, axis=1)
    x = x + _rms(_dot((u * sv).astype(BF16), w_out_ref[...]), g[1:2])
    y_ref[...] = _ffn(x, g[2:3], g[3:4], w_up_ref, w_down_ref)


def _resident(arr, layer=None):
    if layer is None:
        nd = arr.ndim
        return pl.BlockSpec(arr.shape, lambda *_: (0,) * nd, pipeline_mode=pl.Buffered(1))
    nd = arr.ndim - 1
    return pl.BlockSpec((None,) + arr.shape[1:], lambda *_: (layer,) + (0,) * nd,
                        pipeline_mode=pl.Buffered(1))


def _params(n_axes):
    return pltpu.CompilerParams(dimension_semantics=("arbitrary",) * n_axes,
                                vmem_limit_bytes=VMEM_LIMIT_BYTES)


def _split(weights):
    return [a for a, _ in weights], [_resident(a, l) for a, l in weights]


def _layer0_prompt(x, weights):
    b, s, d = x.shape
    tm = TILE_M_PROMPT
    arrs, specs = _split(weights)
    tile = pl.BlockSpec((None, tm, d), lambda i, j: (i, j, 0))
    tail = pl.BlockSpec((None, HALO, D_POOL), lambda i, j: (i, 0, 0))
    return pl.pallas_call(
        _layer0_prompt_kernel,
        grid=(b, s // tm),
        in_specs=[tile] + specs,
        out_specs=[tile, tail, tail],
        out_shape=[jax.ShapeDtypeStruct(x.shape, F32),
                   jax.ShapeDtypeStruct((b, HALO, D_POOL), F32),
                   jax.ShapeDtypeStruct((b, HALO, D_CONV), F32)],
        scratch_shapes=[pltpu.VMEM((HALO, D_POOL), F32), pltpu.VMEM((HALO, D_CONV), F32)],
        compiler_params=_params(2),
        name="layer0_prompt",
    )(x, *arrs)


def _layer0_sample(x, pool_hist, conv_hist, weights, seq):
    n, d = x.shape
    tm = TILE_M_SAMPLE
    tb = tm // seq
    arrs, specs = _split(weights)
    tile = pl.BlockSpec((tm, d), lambda i: (i, 0))
    half = pl.BlockSpec((tm, D_POOL), lambda i: (i, 0))
    hist = pl.BlockSpec((tb, HALO, D_POOL), lambda i: (i, 0, 0))
    return pl.pallas_call(
        functools.partial(_layer0_sample_kernel, seq=seq),
        grid=(n // tm,),
        in_specs=[tile, hist, hist] + specs,
        out_specs=[tile, half, half],
        out_shape=[jax.ShapeDtypeStruct(x.shape, F32),
                   jax.ShapeDtypeStruct((n, D_POOL), F32),
                   jax.ShapeDtypeStruct((n, D_CONV), F32)],
        compiler_params=_params(1),
        name="layer0_sample",
    )(x, pool_hist, conv_hist, *arrs)


def _layer1(x, weights, tm, emit_v, name):
    n, d = x.shape
    arrs, specs = _split(weights)
    tile = pl.BlockSpec((tm, d), lambda i: (i, 0))
    out_specs = [tile, tile] if emit_v else [tile]
    out_shape = [jax.ShapeDtypeStruct(x.shape, F32)] * len(out_specs)
    return pl.pallas_call(
        _layer1_kernel,
        grid=(n // tm,),
        in_specs=[tile] + specs,
        out_specs=out_specs,
        out_shape=out_shape,
        compiler_params=_params(1),
        name=name,
    )(x, *arrs)


def _same_block(n, c):
    r = lax.broadcasted_iota(jnp.int32, (n, n), 0) // c
    return r == lax.broadcasted_iota(jnp.int32, (n, n), 1) // c


def _block_diag(w):
    g, c, _ = w.shape
    return jnp.where(_same_block(g * c, c), jnp.tile(w.reshape(g * c, c), (1, g)), 0)


def _repeat_diag(w, reps):
    s = w.shape[-1]
    return jnp.where(_same_block(reps * s, s)[None], jnp.tile(w, (1, reps, reps)), 0)


def kernel(x_prompt, x_sample, state_pool, state_conv, g_mix_pre, g_mix_post, g_ffn_pre,
           g_ffn_post, w_in_ab, w_pool_grp, pool_scale, conv_w, w_out_ab, w_uv, g_v, w_spatial,
           b_spatial, w_out_c, w_up, w_down):
    bp, sp, d = x_prompt.shape
    bs, ss, _ = x_sample.shape

    gains = jnp.stack([g_mix_pre, g_mix_post, g_ffn_pre, g_ffn_post], axis=1)
    w_up_b = w_up.astype(BF16)
    w_down_b = w_down.astype(BF16)
    w0 = ((gains, 0), (w_in_ab.astype(BF16), 0), (_block_diag(w_pool_grp[0]).astype(BF16), None),
          (pool_scale, None), (conv_w, 0), (w_out_ab.astype(BF16), 0), (w_up_b, 0), (w_down_b, 0))

    bias_p = jnp.repeat(b_spatial[0].T, SGU_HD, axis=1)
    reps = CHUNK // ss
    w_sp_s = _repeat_diag(w_spatial[0][:, :ss, :ss], reps)
    bias_s = jnp.tile(bias_p[:ss], (reps, 1))
    w1_common = ((gains, 1), (w_uv.astype(BF16), 0), (g_v, None))
    w1_tail = ((w_out_c.astype(BF16), 0), (w_up_b, 1), (w_down_b, 1))

    y0, u_tail, cx_tail = _layer0_prompt(x_prompt, w0)
    (y_prompt,) = _layer1(y0.reshape(bp * sp, d),
                          w1_common + ((w_spatial, 0), (bias_p, None)) + w1_tail,
                          TILE_M_PROMPT, False, "layer1_prompt")
    y_prompt = y_prompt.reshape(bp, sp, d)
    pool_p = u_tail[:, HALO - POOL_PREV:][None]
    conv_p = cx_tail[:, HALO - (CONV_W - 1):][None]

    pool_hist = jnp.pad(state_pool[0], ((0, 0), (HALO - POOL_PREV, 0), (0, 0)))
    conv_hist = jnp.pad(state_conv[0], ((0, 0), (HALO - (CONV_W - 1), 0), (0, 0)))
    ys0, u_s, cx_s = _layer0_sample(x_sample.reshape(bs * ss, d), pool_hist, conv_hist, w0, ss)
    ys1, v_s = _layer1(ys0, w1_common + ((w_sp_s, None), (bias_s, None)) + w1_tail,
                       TILE_M_SAMPLE, True, "layer1_sample")
    y_sample = ys1.reshape(bs, ss, d)
    u_s = u_s.reshape(bs, ss, D_POOL)
    cx_s = cx_s.reshape(bs, ss, D_CONV)
    pool_s = jnp.concatenate([state_pool[0][:, ss:], u_s], axis=1)[None]
    conv_s = cx_s[:, ss - (CONV_W - 1):][None]
    v_s = v_s.reshape(bs, ss, D_SGU)[None]
    return (y_prompt, y_sample, pool_p, pool_s, conv_p, conv_s, v_s)
```

```python
import functools

import jax
import jax.numpy as jnp
from jax import lax
from jax.experimental import pallas as pl
from jax.experimental.pallas import tpu as pltpu

D_MODEL = 1024
EPS = 1e-6
D_POOL = D_MODEL // 2
POOL_WINDOWS = (2, 4, 8, 16)
POOL_GC = D_POOL // len(POOL_WINDOWS)
POOL_PREV = max(POOL_WINDOWS) - 1
D_CONV = D_MODEL // 2
CONV_W = 3
D_SGU = D_MODEL
SGU_HEADS = 8
SGU_HD = D_SGU // SGU_HEADS
CHUNK = 128
PAST_LEN = 16384

HALO = 16
TILE_M_PROMPT = 512
TILE_M_SAMPLE = 512
VMEM_LIMIT_BYTES = 60 * 1024 * 1024

F32 = jnp.float32
BF16 = jnp.bfloat16


def _rms(x, g):
    return x * lax.rsqrt(jnp.mean(x * x, axis=-1, keepdims=True) + EPS) * g


def _dot(a, b):
    return jnp.dot(a, b, preferred_element_type=F32)


def _shift_rows(a, k):
    return pltpu.roll(a, k, axis=0)


def _window_sums(full_u):
    sums = []
    for g, w in enumerate(POOL_WINDOWS):
        s = full_u[:, g * POOL_GC:(g + 1) * POOL_GC]
        k = 1
        while k < w:
            s = s + _shift_rows(s, k)
            k *= 2
        sums.append(s)
    return sums


def _pool_out(sums, u, pos, w_grp_ref, scale):
    outs = []
    for g, w in enumerate(POOL_WINDOWS):
        inv_cnt = 1.0 / jnp.minimum(pos + 1.0, float(w))
        d = sums[g] * inv_cnt - u[:, g * POOL_GC:(g + 1) * POOL_GC]
        outs.append(_dot(d.astype(BF16), w_grp_ref[g].astype(BF16)))
    return jnp.concatenate(outs, axis=1) * scale


def _conv_rows(full_cx, conv_w):
    return (conv_w[0:1] * _shift_rows(full_cx, 2) + conv_w[1:2] * _shift_rows(full_cx, 1)
            + conv_w[2:3] * full_cx)


def _gains(gain_refs, layer):
    return [r[layer:layer + 1, :] for r in gain_refs]


def _in_proj(x, g_pre, w_in_ref):
    p = _dot(_rms(x, g_pre).astype(BF16), w_in_ref[...])
    u = p[:, :D_POOL]
    cx = p[:, D_POOL + 2 * D_CONV:] * p[:, D_POOL:D_POOL + D_CONV]
    gate_b = p[:, D_POOL + D_CONV:D_POOL + 2 * D_CONV]
    return u, cx, gate_b


def _ffn(x, g_pre, g_post, w_up_ref, w_down_ref):
    h = _rms(x, g_pre).astype(BF16)
    a = jnp.maximum(_dot(h, w_up_ref[...]), 0.0)
    a = (a * a).astype(BF16)
    f = _dot(a, w_down_ref[...])
    return x + _rms(f, g_post)


def _layer0_prompt_kernel(x_ref, g0, g1, g2, g3, w_grp_ref, scale_ref, conv_w_ref,
                          w_in_ref, w_out_ref, w_up_ref, w_down_ref,
                          y_ref, pool_out_ref, conv_out_ref,
                          halo_u, halo_cx, *, layer):
    tm = x_ref.shape[0]
    t = pl.program_id(1)
    g_pre, g_post, gf_pre, gf_post = _gains((g0, g1, g2, g3), layer)

    @pl.when(t == 0)
    def _():
        halo_u[...] = jnp.zeros_like(halo_u)
        halo_cx[...] = jnp.zeros_like(halo_cx)

    x = x_ref[...]
    u, cx, gate_b = _in_proj(x, g_pre, w_in_ref)
    full_u = jnp.concatenate([halo_u[...], u], axis=0)
    full_cx = jnp.concatenate([halo_cx[...], cx], axis=0)
    halo_u[...] = u[tm - HALO:]
    halo_cx[...] = cx[tm - HALO:]
    pool_out_ref[...] = halo_u[HALO - POOL_PREV:, :]
    conv_out_ref[...] = halo_cx[HALO - (CONV_W - 1):, :]

    pos = (lax.broadcasted_iota(jnp.int32, (tm, 1), 0) + t * tm).astype(F32)
    sums = [s[HALO:] for s in _window_sums(full_u)]
    ya = _pool_out(sums, u, pos, w_grp_ref, scale_ref[...])
    cz = _conv_rows(full_cx, conv_w_ref[...])[HALO:]
    yab = jnp.concatenate([ya, gate_b * cz], axis=1).astype(BF16)
    x = x + _rms(_dot(yab, w_out_ref[...]), g_post)
    y_ref[...] = _ffn(x, gf_pre, gf_post, w_up_ref, w_down_ref)


def _layer0_sample_kernel(x_ref, pool_ref, conv_ref, g0, g1, g2, g3, w_grp_ref, scale_ref,
                          conv_w_ref, w_in_ref, w_out_ref, w_up_ref, w_down_ref,
                          y_ref, pool_out_ref, conv_out_ref,
                          full_u_scr, full_cx_scr, *, layer, seq):
    tm = x_ref.shape[0]
    tb = tm // seq
    grp = HALO + seq
    g_pre, g_post, gf_pre, gf_post = _gains((g0, g1, g2, g3), layer)

    x = x_ref[...]
    u, cx, gate_b = _in_proj(x, g_pre, w_in_ref)

    full_u_scr[:, :HALO - POOL_PREV, :] = jnp.zeros((tb, HALO - POOL_PREV, D_POOL), F32)
    full_u_scr[:, HALO - POOL_PREV:HALO, :] = pool_ref[...]
    full_u_scr[:, HALO:, :] = u.reshape(tb, seq, D_POOL)
    full_cx_scr[:, :HALO - (CONV_W - 1), :] = jnp.zeros((tb, HALO - (CONV_W - 1), D_CONV), F32)
    full_cx_scr[:, HALO - (CONV_W - 1):HALO, :] = conv_ref[...]
    full_cx_scr[:, HALO:, :] = cx.reshape(tb, seq, D_CONV)
    pool_out_ref[...] = full_u_scr[:, grp - POOL_PREV:, :]
    conv_out_ref[...] = full_cx_scr[:, grp - (CONV_W - 1):, :]
    full_u = full_u_scr[...].reshape(tb * grp, D_POOL)
    full_cx = full_cx_scr[...].reshape(tb * grp, D_CONV)

    def new_rows(a):
        return a.reshape(tb, grp, a.shape[-1])[:, HALO:].reshape(tm, a.shape[-1])

    pos = (lax.broadcasted_iota(jnp.int32, (tb, seq, 1), 1).reshape(tm, 1) + PAST_LEN).astype(F32)
    sums = [new_rows(s) for s in _window_sums(full_u)]
    ya = _pool_out(sums, u, pos, w_grp_ref, scale_ref[...])
    cz = new_rows(_conv_rows(full_cx, conv_w_ref[...]))
    yab = jnp.concatenate([ya, gate_b * cz], axis=1).astype(BF16)
    x = x + _rms(_dot(yab, w_out_ref[...]), g_post)
    y_ref[...] = _ffn(x, gf_pre, gf_post, w_up_ref, w_down_ref)


def _sgu_tables(w_sp_ref, b_sp_ref, wm_scr, bias_scr, seq):
    r = lax.broadcasted_iota(jnp.int32, (CHUNK, CHUNK), 0)
    c = lax.broadcasted_iota(jnp.int32, (CHUNK, CHUNK), 1)
    keep = r >= c
    b = b_sp_ref[...]
    if seq is not None:
        keep = keep & (r // seq == c // seq)
        pick = (r % seq == c).astype(BF16)
        pick_t = (c % seq == r).astype(BF16)
        b = jnp.where(lax.broadcasted_iota(jnp.int32, b.shape, 1) < seq, b, 0.0)
        k = seq
        while k < CHUNK:
            b = b + pltpu.roll(b, k, axis=1)
            k *= 2
    mask = keep.astype(F32)
    for hh in range(SGU_HEADS):
        w = w_sp_ref[hh]
        if seq is not None:
            w = _dot(_dot(pick, w.astype(BF16)).astype(BF16), pick_t)
        wm_scr[hh] = (w * mask).astype(BF16)
        rows = jnp.broadcast_to(b[hh:hh + 1, :], (CHUNK, SGU_HD))
        bias_scr[:, hh * SGU_HD:(hh + 1) * SGU_HD] = rows.T


def _layer1_kernel(x_ref, g0, g1, g2, g3, g_v_ref, w_sp_ref, b_sp_ref,
                   w_uv_ref, w_out_ref, w_up_ref, w_down_ref, y_ref, *rest, layer, seq, emit_v):
    if emit_v:
        v_ref, wm_scr, bias_scr = rest
    else:
        wm_scr, bias_scr = rest
    tm = x_ref.shape[0]
    g_pre, g_post, gf_pre, gf_post = _gains((g0, g1, g2, g3), layer)

    @pl.when(pl.program_id(0) == 0)
    def _():
        _sgu_tables(w_sp_ref, b_sp_ref, wm_scr, bias_scr, seq)

    x = x_ref[...]
    p = _dot(_rms(x, g_pre).astype(BF16), w_uv_ref[...])
    u = p[:, :D_SGU]
    v = _rms(p[:, D_SGU:], g_v_ref[...])
    if emit_v:
        v_ref[...] = v
    vb = v.astype(BF16)

    bias = bias_scr[...]
    cols = []
    for hh in range(SGU_HEADS):
        wm = wm_scr[hh]
        lanes = slice(hh * SGU_HD, (hh + 1) * SGU_HD)
        blocks = [_dot(wm, vb[ci * CHUNK:(ci + 1) * CHUNK, lanes]) + bias[:, lanes]
                  for ci in range(tm // CHUNK)]
        cols.append(jnp.concatenate(blocks, axis=0))
    sv = jnp.concatenate(cols, axis=1)
    x = x + _rms(_dot((u * sv).astype(BF16), w_out_ref[...]), g_post)
    y_ref[...] = _ffn(x, gf_pre, gf_post, w_up_ref, w_down_ref)


def _resident(arr, layer=None):
    if layer is None:
        nd = arr.ndim
        return pl.BlockSpec(arr.shape, lambda *_: (0,) * nd, pipeline_mode=pl.Buffered(1))
    nd = arr.ndim - 1
    return pl.BlockSpec((None,) + arr.shape[1:], lambda *_: (layer,) + (0,) * nd,
                        pipeline_mode=pl.Buffered(1))


def _params(n_axes):
    return pltpu.CompilerParams(dimension_semantics=("arbitrary",) * n_axes,
                                vmem_limit_bytes=VMEM_LIMIT_BYTES)


def _split(weights):
    return [a for a, _ in weights], [_resident(a, l) for a, l in weights]


def _layer0_prompt(x, weights, layer):
    b, s, d = x.shape
    tm = TILE_M_PROMPT
    arrs, specs = _split(weights)
    tile = pl.BlockSpec((None, tm, d), lambda i, j: (i, j, 0))
    pool = pl.BlockSpec((None, None, POOL_PREV, D_POOL), lambda i, j: (0, i, 0, 0))
    conv = pl.BlockSpec((None, None, CONV_W - 1, D_CONV), lambda i, j: (0, i, 0, 0))
    return pl.pallas_call(
        functools.partial(_layer0_prompt_kernel, layer=layer),
        grid=(b, s // tm),
        in_specs=[tile] + specs,
        out_specs=[tile, pool, conv],
        out_shape=[jax.ShapeDtypeStruct(x.shape, F32),
                   jax.ShapeDtypeStruct((1, b, POOL_PREV, D_POOL), F32),
                   jax.ShapeDtypeStruct((1, b, CONV_W - 1, D_CONV), F32)],
        scratch_shapes=[pltpu.VMEM((HALO, D_POOL), F32), pltpu.VMEM((HALO, D_CONV), F32)],
        compiler_params=_params(2),
        name="layer0_prompt",
    )(x, *arrs)


def _layer0_sample(x, state_pool, state_conv, weights, layer, seq):
    n, d = x.shape
    tm = TILE_M_SAMPLE
    tb = tm // seq
    arrs, specs = _split(weights)
    tile = pl.BlockSpec((tm, d), lambda i: (i, 0))
    pool = pl.BlockSpec((None, tb, POOL_PREV, D_POOL), lambda i: (0, i, 0, 0))
    conv = pl.BlockSpec((None, tb, CONV_W - 1, D_CONV), lambda i: (0, i, 0, 0))
    return pl.pallas_call(
        functools.partial(_layer0_sample_kernel, layer=layer, seq=seq),
        grid=(n // tm,),
        in_specs=[tile, pool, conv] + specs,
        out_specs=[tile, pool, conv],
        out_shape=[jax.ShapeDtypeStruct(x.shape, F32),
                   jax.ShapeDtypeStruct(state_pool.shape, F32),
                   jax.ShapeDtypeStruct(state_conv.shape, F32)],
        scratch_shapes=[pltpu.VMEM((tb, HALO + seq, D_POOL), F32),
                        pltpu.VMEM((tb, HALO + seq, D_CONV), F32)],
        compiler_params=_params(1),
        name="layer0_sample",
    )(x, state_pool, state_conv, *arrs)


def _layer1(x, weights, tm, layer, seq, emit_v, name):
    n, d = x.shape
    arrs, specs = _split(weights)
    tile = pl.BlockSpec((tm, d), lambda i: (i, 0))
    out_specs = [tile, tile] if emit_v else [tile]
    out_shape = [jax.ShapeDtypeStruct(x.shape, F32)] * len(out_specs)
    return pl.pallas_call(
        functools.partial(_layer1_kernel, layer=layer, seq=seq, emit_v=emit_v),
        grid=(n // tm,),
        in_specs=[tile] + specs,
        out_specs=out_specs,
        out_shape=out_shape,
        scratch_shapes=[pltpu.VMEM((SGU_HEADS, CHUNK, CHUNK), BF16),
                        pltpu.VMEM((CHUNK, D_SGU), F32)],
        compiler_params=_params(1),
        name=name,
    )(x, *arrs)


def kernel(x_prompt, x_sample, state_pool, state_conv, g_mix_pre, g_mix_post, g_ffn_pre,
           g_ffn_post, w_in_ab, w_pool_grp, pool_scale, conv_w, w_out_ab, w_uv, g_v, w_spatial,
           b_spatial, w_out_c, w_up, w_down):
    bp, sp, d = x_prompt.shape
    bs, ss, _ = x_sample.shape
    assert ss < CHUNK and CHUNK % ss == 0 and CONV_W - 1 <= ss and sp % CHUNK == 0

    gains = ((g_mix_pre, None), (g_mix_post, None), (g_ffn_pre, None), (g_ffn_post, None))
    w_up_b = w_up.astype(BF16)
    w_down_b = w_down.astype(BF16)
    w0 = gains + ((w_pool_grp, 0), (pool_scale, None), (conv_w, 0),
                  (w_in_ab.astype(BF16), 0), (w_out_ab.astype(BF16), 0),
                  (w_up_b, 0), (w_down_b, 0))
    w1 = gains + ((g_v, None), (w_spatial, 0), (b_spatial, 0),
                  (w_uv.astype(BF16), 0), (w_out_c.astype(BF16), 0), (w_up_b, 1), (w_down_b, 1))

    y0, pool_p, conv_p = _layer0_prompt(x_prompt, w0, 0)
    (y_prompt,) = _layer1(y0.reshape(bp * sp, d), w1, TILE_M_PROMPT, 1, None, False,
                          "layer1_prompt")

    ys0, pool_s, conv_s = _layer0_sample(x_sample.reshape(bs * ss, d), state_pool, state_conv,
                                         w0, 0, ss)
    ys1, v_s = _layer1(ys0, w1, TILE_M_SAMPLE, 1, ss, True, "layer1_sample")
    return (y_prompt.reshape(bp, sp, d), ys1.reshape(bs, ss, d), pool_p, pool_s, conv_p, conv_s,
            v_s.reshape(1, bs, ss, D_SGU))
```

```python
import functools

import jax
import jax.numpy as jnp
from jax import lax
from jax.experimental import pallas as pl
from jax.experimental.pallas import tpu as pltpu

D_MODEL = 1024
EPS = 1e-6
D_POOL = D_MODEL // 2
POOL_WINDOWS = (2, 4, 8, 16)
POOL_GC = D_POOL // len(POOL_WINDOWS)
POOL_PREV = max(POOL_WINDOWS) - 1
D_CONV = D_MODEL // 2
CONV_W = 3
D_SGU = D_MODEL
SGU_HEADS = 8
SGU_HD = D_SGU // SGU_HEADS
CHUNK = 128
PAST_LEN = 16384

HALO = 16
TILE_M_PROMPT = 512
TILE_M_SAMPLE = 512
VMEM_LIMIT_BYTES = 60 * 1024 * 1024

F32 = jnp.float32
BF16 = jnp.bfloat16


def _rms(x, g):
    return x * lax.rsqrt(jnp.mean(x * x, axis=-1, keepdims=True) + EPS) * g


def _dot(a, b):
    return jnp.dot(a, b, preferred_element_type=F32)


def _shift_rows(a, k):
    return pltpu.roll(a, k, axis=0)


def _window_sums(full_u):
    sums = []
    for g, w in enumerate(POOL_WINDOWS):
        s = full_u[:, g * POOL_GC:(g + 1) * POOL_GC]
        k = 1
        while k < w:
            s = s + _shift_rows(s, k)
            k *= 2
        sums.append(s)
    return sums


def _pool_diff(sums, u, pos):
    outs = []
    for g, w in enumerate(POOL_WINDOWS):
        inv_cnt = 1.0 / jnp.minimum(pos + 1.0, float(w))
        outs.append(sums[g] * inv_cnt - u[:, g * POOL_GC:(g + 1) * POOL_GC])
    return jnp.concatenate(outs, axis=1)


def _fold_pool_into_out(w_grp_ref, scale_ref, w_out_ref, w_eff_scr):
    scale = scale_ref[...]
    for g in range(len(POOL_WINDOWS)):
        rows = slice(g * POOL_GC, (g + 1) * POOL_GC)
        a = w_grp_ref[g] * scale[:, rows]
        b = w_out_ref[rows, :].astype(F32)
        w_eff_scr[rows, :] = jnp.dot(a, b, precision=lax.Precision.HIGHEST,
                                     preferred_element_type=F32).astype(BF16)
    w_eff_scr[D_POOL:, :] = w_out_ref[D_POOL:, :]


def _conv_rows(full_cx, conv_w):
    return (conv_w[0:1] * _shift_rows(full_cx, 2) + conv_w[1:2] * _shift_rows(full_cx, 1)
            + conv_w[2:3] * full_cx)


def _gains(gain_refs, layer):
    return [r[layer:layer + 1, :] for r in gain_refs]


def _in_proj(x, g_pre, w_in_ref):
    p = _dot(_rms(x, g_pre).astype(BF16), w_in_ref[...])
    u = p[:, :D_POOL]
    cx = p[:, D_POOL + 2 * D_CONV:] * p[:, D_POOL:D_POOL + D_CONV]
    gate_b = p[:, D_POOL + D_CONV:D_POOL + 2 * D_CONV]
    return u, cx, gate_b


def _ffn(x, g_pre, g_post, w_up_ref, w_down_ref):
    h = _rms(x, g_pre).astype(BF16)
    a = jnp.maximum(_dot(h, w_up_ref[...]), 0.0)
    a = (a * a).astype(BF16)
    f = _dot(a, w_down_ref[...])
    return x + _rms(f, g_post)


def _layer0_prompt_kernel(x_ref, g0, g1, g2, g3, w_grp_ref, scale_ref, conv_w_ref,
                          w_in_ref, w_out_ref, w_up_ref, w_down_ref, *rest, layer, n_cast):
    cast_in, rest = rest[:n_cast], rest[n_cast:]
    y_ref, pool_out_ref, conv_out_ref = rest[:3]
    cast_out, (halo_u, halo_cx, w_eff_scr) = rest[3:3 + n_cast], rest[3 + n_cast:]
    tm = x_ref.shape[0]
    t = pl.program_id(1)
    g_pre, g_post, gf_pre, gf_post = _gains((g0, g1, g2, g3), layer)

    for src, dst in zip(cast_in, cast_out):
        dst[...] = src[...].astype(BF16)

    @pl.when((pl.program_id(0) == 0) & (t == 0))
    def _():
        _fold_pool_into_out(w_grp_ref, scale_ref, w_out_ref, w_eff_scr)

    @pl.when(t == 0)
    def _():
        halo_u[...] = jnp.zeros_like(halo_u)
        halo_cx[...] = jnp.zeros_like(halo_cx)

    x = x_ref[...]
    u, cx, gate_b = _in_proj(x, g_pre, w_in_ref)
    full_u = jnp.concatenate([halo_u[...], u], axis=0)
    full_cx = jnp.concatenate([halo_cx[...], cx], axis=0)
    halo_u[...] = u[tm - HALO:]
    halo_cx[...] = cx[tm - HALO:]
    pool_out_ref[...] = halo_u[HALO - POOL_PREV:, :]
    conv_out_ref[...] = halo_cx[HALO - (CONV_W - 1):, :]

    pos = (lax.broadcasted_iota(jnp.int32, (tm, 1), 0) + t * tm).astype(F32)
    sums = [s[HALO:] for s in _window_sums(full_u)]
    d = _pool_diff(sums, u, pos)
    cz = _conv_rows(full_cx, conv_w_ref[...])[HALO:]
    yab = jnp.concatenate([d, gate_b * cz], axis=1).astype(BF16)
    x = x + _rms(_dot(yab, w_eff_scr[...]), g_post)
    y_ref[...] = _ffn(x, gf_pre, gf_post, w_up_ref, w_down_ref)


def _layer0_sample_kernel(x_ref, pool_ref, conv_ref, g0, g1, g2, g3, w_grp_ref, scale_ref,
                          conv_w_ref, w_in_ref, w_out_ref, w_up_ref, w_down_ref,
                          y_ref, pool_out_ref, conv_out_ref,
                          full_u_scr, full_cx_scr, w_eff_scr, *, layer, seq):
    tm = x_ref.shape[0]
    tb = tm // seq
    grp = HALO + seq
    g_pre, g_post, gf_pre, gf_post = _gains((g0, g1, g2, g3), layer)

    @pl.when(pl.program_id(0) == 0)
    def _():
        _fold_pool_into_out(w_grp_ref, scale_ref, w_out_ref, w_eff_scr)

    x = x_ref[...]
    u, cx, gate_b = _in_proj(x, g_pre, w_in_ref)

    full_u_scr[:, :HALO - POOL_PREV, :] = jnp.zeros((tb, HALO - POOL_PREV, D_POOL), F32)
    full_u_scr[:, HALO - POOL_PREV:HALO, :] = pool_ref[...]
    full_u_scr[:, HALO:, :] = u.reshape(tb, seq, D_POOL)
    full_cx_scr[:, :HALO - (CONV_W - 1), :] = jnp.zeros((tb, HALO - (CONV_W - 1), D_CONV), F32)
    full_cx_scr[:, HALO - (CONV_W - 1):HALO, :] = conv_ref[...]
    full_cx_scr[:, HALO:, :] = cx.reshape(tb, seq, D_CONV)
    pool_out_ref[...] = full_u_scr[:, grp - POOL_PREV:, :]
    conv_out_ref[...] = full_cx_scr[:, grp - (CONV_W - 1):, :]
    full_u = full_u_scr[...].reshape(tb * grp, D_POOL)
    full_cx = full_cx_scr[...].reshape(tb * grp, D_CONV)

    def new_rows(a):
        return a.reshape(tb, grp, a.shape[-1])[:, HALO:].reshape(tm, a.shape[-1])

    pos = (lax.broadcasted_iota(jnp.int32, (tb, seq, 1), 1).reshape(tm, 1) + PAST_LEN).astype(F32)
    sums = [new_rows(s) for s in _window_sums(full_u)]
    d = _pool_diff(sums, u, pos)
    cz = new_rows(_conv_rows(full_cx, conv_w_ref[...]))
    yab = jnp.concatenate([d, gate_b * cz], axis=1).astype(BF16)
    x = x + _rms(_dot(yab, w_eff_scr[...]), g_post)
    y_ref[...] = _ffn(x, gf_pre, gf_post, w_up_ref, w_down_ref)


def _sgu_tables(w_sp_ref, b_sp_ref, wm_scr, bias_scr, seq):
    r = lax.broadcasted_iota(jnp.int32, (CHUNK, CHUNK), 0)
    c = lax.broadcasted_iota(jnp.int32, (CHUNK, CHUNK), 1)
    keep = r >= c
    b = b_sp_ref[...]
    if seq is not None:
        keep = keep & (r // seq == c // seq)
        pick = (r % seq == c).astype(BF16)
        pick_t = (c % seq == r).astype(BF16)
        b = jnp.where(lax.broadcasted_iota(jnp.int32, b.shape, 1) < seq, b, 0.0)
        k = seq
        while k < CHUNK:
            b = b + pltpu.roll(b, k, axis=1)
            k *= 2
    mask = keep.astype(F32)
    for hh in range(SGU_HEADS):
        w = w_sp_ref[hh]
        if seq is not None:
            w = _dot(_dot(pick, w.astype(BF16)).astype(BF16), pick_t)
        wm_scr[hh] = (w * mask).astype(BF16)
        rows = jnp.broadcast_to(b[hh:hh + 1, :], (CHUNK, SGU_HD))
        bias_scr[:, hh * SGU_HD:(hh + 1) * SGU_HD] = rows.T


def _layer1_kernel(x_ref, g0, g1, g2, g3, g_v_ref, w_sp_ref, b_sp_ref,
                   w_uv_ref, w_out_ref, w_up_ref, w_down_ref, y_ref, *rest, layer, seq, emit_v):
    if emit_v:
        v_ref, wm_scr, bias_scr = rest
    else:
        wm_scr, bias_scr = rest
    tm = x_ref.shape[0]
    g_pre, g_post, gf_pre, gf_post = _gains((g0, g1, g2, g3), layer)

    @pl.when(pl.program_id(0) == 0)
    def _():
        _sgu_tables(w_sp_ref, b_sp_ref, wm_scr, bias_scr, seq)

    x = x_ref[...]
    p = _dot(_rms(x, g_pre).astype(BF16), w_uv_ref[...])
    u = p[:, :D_SGU]
    v = _rms(p[:, D_SGU:], g_v_ref[...])
    if emit_v:
        v_ref[...] = v
    vb = v.astype(BF16)

    bias = bias_scr[...]
    nc = tm // CHUNK
    cols = []
    for hh in range(SGU_HEADS):
        lanes = slice(hh * SGU_HD, (hh + 1) * SGU_HD)
        rhs = jnp.concatenate([vb[ci * CHUNK:(ci + 1) * CHUNK, lanes] for ci in range(nc)], axis=1)
        out = _dot(wm_scr[hh], rhs)
        cols.append(jnp.concatenate(
            [out[:, ci * SGU_HD:(ci + 1) * SGU_HD] + bias[:, lanes] for ci in range(nc)], axis=0))
    sv = jnp.concatenate(cols, axis=1)
    x = x + _rms(_dot((u * sv).astype(BF16), w_out_ref[...]), g_post)
    y_ref[...] = _ffn(x, gf_pre, gf_post, w_up_ref, w_down_ref)


def _resident(arr, layer=None):
    if layer is None:
        nd = arr.ndim
        return pl.BlockSpec(arr.shape, lambda *_: (0,) * nd, pipeline_mode=pl.Buffered(1))
    nd = arr.ndim - 1
    return pl.BlockSpec((None,) + arr.shape[1:], lambda *_: (layer,) + (0,) * nd,
                        pipeline_mode=pl.Buffered(1))


def _params(n_axes):
    return pltpu.CompilerParams(dimension_semantics=("arbitrary",) * n_axes,
                                vmem_limit_bytes=VMEM_LIMIT_BYTES)


def _split(weights):
    return [a for a, _ in weights], [_resident(a, l) for a, l in weights]


def _layer0_prompt(x, weights, layer, to_cast):
    b, s, d = x.shape
    tm = TILE_M_PROMPT
    nt = s // tm
    steps = b * nt
    arrs, specs = _split(weights)
    tile = pl.BlockSpec((None, tm, d), lambda i, j: (i, j, 0))
    pool = pl.BlockSpec((None, None, POOL_PREV, D_POOL), lambda i, j: (0, i, 0, 0))
    conv = pl.BlockSpec((None, None, CONV_W - 1, D_CONV), lambda i, j: (0, i, 0, 0))
    cast_in, cast_out, cast_shapes = [], [], []
    for w, slab in to_cast:
        _, rows, cols = w.shape
        assert rows % steps == 0
        cast_in.append(pl.BlockSpec((None, rows // steps, cols),
                                    lambda i, j, slab=slab: (slab, i * nt + j, 0)))
        cast_out.append(pl.BlockSpec((rows // steps, cols), lambda i, j: (i * nt + j, 0)))
        cast_shapes.append(jax.ShapeDtypeStruct((rows, cols), BF16))
    return pl.pallas_call(
        functools.partial(_layer0_prompt_kernel, layer=layer, n_cast=len(to_cast)),
        grid=(b, nt),
        in_specs=[tile] + specs + cast_in,
        out_specs=[tile, pool, conv] + cast_out,
        out_shape=[jax.ShapeDtypeStruct(x.shape, F32),
                   jax.ShapeDtypeStruct((1, b, POOL_PREV, D_POOL), F32),
                   jax.ShapeDtypeStruct((1, b, CONV_W - 1, D_CONV), F32)] + cast_shapes,
        scratch_shapes=[pltpu.VMEM((HALO, D_POOL), F32), pltpu.VMEM((HALO, D_CONV), F32),
                        pltpu.VMEM((D_POOL + D_CONV, d), BF16)],
        compiler_params=_params(2),
        name="layer0_prompt",
    )(x, *arrs, *[w for w, _ in to_cast])


def _layer0_sample(x, state_pool, state_conv, weights, layer, seq):
    n, d = x.shape
    tm = TILE_M_SAMPLE
    tb = tm // seq
    arrs, specs = _split(weights)
    tile = pl.BlockSpec((tm, d), lambda i: (i, 0))
    pool = pl.BlockSpec((None, tb, POOL_PREV, D_POOL), lambda i: (0, i, 0, 0))
    conv = pl.BlockSpec((None, tb, CONV_W - 1, D_CONV), lambda i: (0, i, 0, 0))
    return pl.pallas_call(
        functools.partial(_layer0_sample_kernel, layer=layer, seq=seq),
        grid=(n // tm,),
        in_specs=[tile, pool, conv] + specs,
        out_specs=[tile, pool, conv],
        out_shape=[jax.ShapeDtypeStruct(x.shape, F32),
                   jax.ShapeDtypeStruct(state_pool.shape, F32),
                   jax.ShapeDtypeStruct(state_conv.shape, F32)],
        scratch_shapes=[pltpu.VMEM((tb, HALO + seq, D_POOL), F32),
                        pltpu.VMEM((tb, HALO + seq, D_CONV), F32),
                        pltpu.VMEM((D_POOL + D_CONV, d), BF16)],
        compiler_params=_params(1),
        name="layer0_sample",
    )(x, state_pool, state_conv, *arrs)


def _layer1(x, weights, tm, layer, seq, emit_v, name):
    n, d = x.shape
    arrs, specs = _split(weights)
    tile = pl.BlockSpec((tm, d), lambda i: (i, 0))
    out_specs = [tile, tile] if emit_v else [tile]
    out_shape = [jax.ShapeDtypeStruct(x.shape, F32)] * len(out_specs)
    return pl.pallas_call(
        functools.partial(_layer1_kernel, layer=layer, seq=seq, emit_v=emit_v),
        grid=(n // tm,),
        in_specs=[tile] + specs,
        out_specs=out_specs,
        out_shape=out_shape,
        scratch_shapes=[pltpu.VMEM((SGU_HEADS, CHUNK, CHUNK), BF16),
                        pltpu.VMEM((CHUNK, D_SGU), F32)],
        compiler_params=_params(1),
        name=name,
    )(x, *arrs)


def kernel(x_prompt, x_sample, state_pool, state_conv, g_mix_pre, g_mix_post, g_ffn_pre,
           g_ffn_post, w_in_ab, w_pool_grp, pool_scale, conv_w, w_out_ab, w_uv, g_v, w_spatial,
           b_spatial, w_out_c, w_up, w_down):
    bp, sp, d = x_prompt.shape
    bs, ss, _ = x_sample.shape
    assert ss < CHUNK and CHUNK % ss == 0 and CONV_W - 1 <= ss and sp % CHUNK == 0

    gains = ((g_mix_pre, None), (g_mix_post, None), (g_ffn_pre, None), (g_ffn_post, None))
    w0 = gains + ((w_pool_grp, 0), (pool_scale, None), (conv_w, 0),
                  (w_in_ab[0].astype(BF16), None), (w_out_ab[0].astype(BF16), None),
                  (w_up[0].astype(BF16), None), (w_down[0].astype(BF16), None))

    y0, pool_p, conv_p, w_uv_b, w_out_c_b, w_up1_b, w_down1_b = _layer0_prompt(
        x_prompt, w0, 0, ((w_uv, 0), (w_out_c, 0), (w_up, 1), (w_down, 1)))
    w1 = gains + ((g_v, None), (w_spatial, 0), (b_spatial, 0),
                  (w_uv_b, None), (w_out_c_b, None), (w_up1_b, None), (w_down1_b, None))
    (y_prompt,) = _layer1(y0.reshape(bp * sp, d), w1, TILE_M_PROMPT, 1, None, False,
                          "layer1_prompt")

    ys0, pool_s, conv_s = _layer0_sample(x_sample.reshape(bs * ss, d), state_pool, state_conv,
                                         w0, 0, ss)
    ys1, v_s = _layer1(ys0, w1, TILE_M_SAMPLE, 1, ss, True, "layer1_sample")
    return (y_prompt.reshape(bp, sp, d), ys1.reshape(bs, ss, d), pool_p, pool_s, conv_p, conv_s,
            v_s.reshape(1, bs, ss, D_SGU))
```

```python
import functools

import jax
import jax.numpy as jnp
from jax import lax
from jax.experimental import pallas as pl
from jax.experimental.pallas import tpu as pltpu

D_MODEL = 1024
EPS = 1e-6
D_POOL = D_MODEL // 2
POOL_WINDOWS = (2, 4, 8, 16)
POOL_GC = D_POOL // len(POOL_WINDOWS)
POOL_PREV = max(POOL_WINDOWS) - 1
D_CONV = D_MODEL // 2
CONV_W = 3
D_SGU = D_MODEL
SGU_HEADS = 8
SGU_HD = D_SGU // SGU_HEADS
CHUNK = 128
PAST_LEN = 16384

HALO = 16
TILE_M_PROMPT = 512
TILE_M_SAMPLE = 512
SUB_TILES = 2
SUB_TILES_SAMPLE0 = 1
VMEM_LIMIT_BYTES = 60 * 1024 * 1024

F32 = jnp.float32
BF16 = jnp.bfloat16


def _rms(x, g):
    return x * lax.rsqrt(jnp.mean(x * x, axis=-1, keepdims=True) + EPS) * g


def _dot(a, b):
    return jnp.dot(a, b, preferred_element_type=F32)


def _shift_rows(a, k):
    return pltpu.roll(a, k, axis=0)


def _window_sums(full_u):
    sums = []
    for g, w in enumerate(POOL_WINDOWS):
        s = full_u[:, g * POOL_GC:(g + 1) * POOL_GC]
        k = 1
        while k < w:
            s = s + _shift_rows(s, k)
            k *= 2
        sums.append(s)
    return sums


def _pool_diff(sums, u, pos):
    outs = []
    for g, w in enumerate(POOL_WINDOWS):
        inv_cnt = 1.0 / jnp.minimum(pos + 1.0, float(w))
        outs.append(sums[g] * inv_cnt - u[:, g * POOL_GC:(g + 1) * POOL_GC])
    return jnp.concatenate(outs, axis=1)


def _fold_pool_into_out(w_grp_ref, scale_ref, w_out_ref, w_eff_scr):
    scale = scale_ref[...]
    for g in range(len(POOL_WINDOWS)):
        rows = slice(g * POOL_GC, (g + 1) * POOL_GC)
        a = w_grp_ref[g] * scale[:, rows]
        b = w_out_ref[rows, :].astype(F32)
        w_eff_scr[rows, :] = jnp.dot(a, b, precision=lax.Precision.HIGHEST,
                                     preferred_element_type=F32).astype(BF16)
    w_eff_scr[D_POOL:, :] = w_out_ref[D_POOL:, :]


def _conv_rows(full_cx, conv_w):
    return (conv_w[0:1] * _shift_rows(full_cx, 2) + conv_w[1:2] * _shift_rows(full_cx, 1)
            + conv_w[2:3] * full_cx)


def _gains(gain_refs, layer):
    return [r[layer:layer + 1, :] for r in gain_refs]


def _in_proj(x, g_pre, w_in_ref):
    p = _dot(_rms(x, g_pre).astype(BF16), w_in_ref[...])
    u = p[:, :D_POOL]
    cx = p[:, D_POOL + 2 * D_CONV:] * p[:, D_POOL:D_POOL + D_CONV]
    gate_b = p[:, D_POOL + D_CONV:D_POOL + 2 * D_CONV]
    return u, cx, gate_b


def _ffn(x, g_pre, g_post, w_up_ref, w_down_ref):
    h = _rms(x, g_pre).astype(BF16)
    a = jnp.maximum(_dot(h, w_up_ref[...]), 0.0)
    a = (a * a).astype(BF16)
    f = _dot(a, w_down_ref[...])
    return x + _rms(f, g_post)


def _ffn_staged(xs, g_pre, g_post, w_up_ref, w_down_ref):
    acts = []
    for x in xs:
        a = jnp.maximum(_dot(_rms(x, g_pre).astype(BF16), w_up_ref[...]), 0.0)
        acts.append((a * a).astype(BF16))
    return [x + _rms(_dot(a, w_down_ref[...]), g_post) for x, a in zip(xs, acts)]


def _layer0_prompt_kernel(x_ref, g0, g1, g2, g3, w_grp_ref, scale_ref, conv_w_ref,
                          w_in_ref, w_out_ref, w_up_ref, w_down_ref, *rest, layer, n_cast):
    cast_in, rest = rest[:n_cast], rest[n_cast:]
    y_ref, pool_out_ref, conv_out_ref = rest[:3]
    cast_out, (halo_u, halo_cx, w_eff_scr) = rest[3:3 + n_cast], rest[3 + n_cast:]
    tm = x_ref.shape[0]
    t = pl.program_id(1)
    g_pre, g_post, gf_pre, gf_post = _gains((g0, g1, g2, g3), layer)

    for src, dst in zip(cast_in, cast_out):
        dst[...] = src[...].astype(BF16)

    @pl.when((pl.program_id(0) == 0) & (t == 0))
    def _():
        _fold_pool_into_out(w_grp_ref, scale_ref, w_out_ref, w_eff_scr)

    @pl.when(t == 0)
    def _():
        halo_u[...] = jnp.zeros_like(halo_u)
        halo_cx[...] = jnp.zeros_like(halo_cx)

    ts = tm // SUB_TILES
    subs = range(SUB_TILES)
    xs = [x_ref[pl.ds(sub * ts, ts), :] for sub in subs]
    proj = [_in_proj(x, g_pre, w_in_ref) for x in xs]

    hist_u, hist_cx = halo_u[...], halo_cx[...]
    yabs = []
    for sub, (u, cx, gate_b) in enumerate(proj):
        full_u = jnp.concatenate([hist_u, u], axis=0)
        full_cx = jnp.concatenate([hist_cx, cx], axis=0)
        hist_u, hist_cx = u[ts - HALO:], cx[ts - HALO:]
        pos = (lax.broadcasted_iota(jnp.int32, (ts, 1), 0) + (t * tm + sub * ts)).astype(F32)
        sums = [s[HALO:] for s in _window_sums(full_u)]
        d = _pool_diff(sums, u, pos)
        cz = _conv_rows(full_cx, conv_w_ref[...])[HALO:]
        yabs.append(jnp.concatenate([d, gate_b * cz], axis=1).astype(BF16))
    halo_u[...] = hist_u
    halo_cx[...] = hist_cx

    x1 = [x + _rms(_dot(yab, w_eff_scr[...]), g_post) for x, yab in zip(xs, yabs)]
    for sub, y in enumerate(_ffn_staged(x1, gf_pre, gf_post, w_up_ref, w_down_ref)):
        y_ref[pl.ds(sub * ts, ts), :] = y

    pool_out_ref[...] = halo_u[HALO - POOL_PREV:, :]
    conv_out_ref[...] = halo_cx[HALO - (CONV_W - 1):, :]


def _layer0_sample_kernel(x_ref, pool_ref, conv_ref, g0, g1, g2, g3, w_grp_ref, scale_ref,
                          conv_w_ref, w_in_ref, w_out_ref, w_up_ref, w_down_ref,
                          y_ref, pool_out_ref, conv_out_ref,
                          full_u_scr, full_cx_scr, w_eff_scr, *, layer, seq):
    tm = x_ref.shape[0]
    tb = tm // seq
    grp = HALO + seq
    g_pre, g_post, gf_pre, gf_post = _gains((g0, g1, g2, g3), layer)

    @pl.when(pl.program_id(0) == 0)
    def _():
        _fold_pool_into_out(w_grp_ref, scale_ref, w_out_ref, w_eff_scr)

    ts = tm // SUB_TILES_SAMPLE0
    tbs = tb // SUB_TILES_SAMPLE0
    subs = range(SUB_TILES_SAMPLE0)
    xs = [x_ref[pl.ds(sub * ts, ts), :] for sub in subs]
    proj = [_in_proj(x, g_pre, w_in_ref) for x in xs]

    def new_rows(a):
        return a.reshape(tbs, grp, a.shape[-1])[:, HALO:].reshape(ts, a.shape[-1])

    pos = (lax.broadcasted_iota(jnp.int32, (tbs, seq, 1), 1).reshape(ts, 1) + PAST_LEN).astype(F32)
    yabs = []
    for sub, (u, cx, gate_b) in enumerate(proj):
        sq = pl.ds(sub * tbs, tbs)
        full_u_scr[sq, :HALO - POOL_PREV, :] = jnp.zeros((tbs, HALO - POOL_PREV, D_POOL), F32)
        full_u_scr[sq, HALO - POOL_PREV:HALO, :] = pool_ref[sq]
        full_u_scr[sq, HALO:, :] = u.reshape(tbs, seq, D_POOL)
        full_cx_scr[sq, :HALO - (CONV_W - 1), :] = jnp.zeros((tbs, HALO - (CONV_W - 1), D_CONV), F32)
        full_cx_scr[sq, HALO - (CONV_W - 1):HALO, :] = conv_ref[sq]
        full_cx_scr[sq, HALO:, :] = cx.reshape(tbs, seq, D_CONV)
        pool_out_ref[sq] = full_u_scr[sq, grp - POOL_PREV:, :]
        conv_out_ref[sq] = full_cx_scr[sq, grp - (CONV_W - 1):, :]
        full_u = full_u_scr[sq].reshape(tbs * grp, D_POOL)
        full_cx = full_cx_scr[sq].reshape(tbs * grp, D_CONV)
        sums = [new_rows(s) for s in _window_sums(full_u)]
        d = _pool_diff(sums, u, pos)
        cz = new_rows(_conv_rows(full_cx, conv_w_ref[...]))
        yabs.append(jnp.concatenate([d, gate_b * cz], axis=1).astype(BF16))

    x1 = [x + _rms(_dot(yab, w_eff_scr[...]), g_post) for x, yab in zip(xs, yabs)]
    for sub, y in enumerate(_ffn_staged(x1, gf_pre, gf_post, w_up_ref, w_down_ref)):
        y_ref[pl.ds(sub * ts, ts), :] = y


def _sgu_tables(w_sp_ref, b_sp_ref, wm_scr, bias_scr, seq):
    r = lax.broadcasted_iota(jnp.int32, (CHUNK, CHUNK), 0)
    c = lax.broadcasted_iota(jnp.int32, (CHUNK, CHUNK), 1)
    keep = r >= c
    b = b_sp_ref[...]
    if seq is not None:
        keep = keep & (r // seq == c // seq)
        pick = (r % seq == c).astype(BF16)
        pick_t = (c % seq == r).astype(BF16)
        b = jnp.where(lax.broadcasted_iota(jnp.int32, b.shape, 1) < seq, b, 0.0)
        k = seq
        while k < CHUNK:
            b = b + pltpu.roll(b, k, axis=1)
            k *= 2
    mask = keep.astype(F32)
    for hh in range(SGU_HEADS):
        w = w_sp_ref[hh]
        if seq is not None:
            w = _dot(_dot(pick, w.astype(BF16)).astype(BF16), pick_t)
        wm_scr[hh] = (w * mask).astype(BF16)
        rows = jnp.broadcast_to(b[hh:hh + 1, :], (CHUNK, SGU_HD))
        bias_scr[:, hh * SGU_HD:(hh + 1) * SGU_HD] = rows.T


def _layer1_kernel(x_ref, g0, g1, g2, g3, g_v_ref, w_sp_ref, b_sp_ref,
                   w_uv_ref, w_out_ref, w_up_ref, w_down_ref, y_ref, *rest, layer, seq, emit_v):
    if emit_v:
        v_ref, wm_scr, bias_scr = rest
    else:
        wm_scr, bias_scr = rest
    tm = x_ref.shape[0]
    g_pre, g_post, gf_pre, gf_post = _gains((g0, g1, g2, g3), layer)

    @pl.when(pl.program_id(0) == 0)
    def _():
        _sgu_tables(w_sp_ref, b_sp_ref, wm_scr, bias_scr, seq)

    ts = tm // SUB_TILES
    nc = ts // CHUNK
    subs = range(SUB_TILES)
    xs = [x_ref[pl.ds(sub * ts, ts), :] for sub in subs]
    ps = [_dot(_rms(x, g_pre).astype(BF16), w_uv_ref[...]) for x in xs]

    bias = bias_scr[...]
    gated = []
    for sub, p in enumerate(ps):
        u = p[:, :D_SGU]
        v = _rms(p[:, D_SGU:], g_v_ref[...])
        if emit_v:
            v_ref[pl.ds(sub * ts, ts), :] = v
        vb = v.astype(BF16)
        cols = []
        for hh in range(SGU_HEADS):
            lanes = slice(hh * SGU_HD, (hh + 1) * SGU_HD)
            rhs = jnp.concatenate([vb[ci * CHUNK:(ci + 1) * CHUNK, lanes] for ci in range(nc)],
                                  axis=1)
            out = _dot(wm_scr[hh], rhs)
            cols.append(jnp.concatenate(
                [out[:, ci * SGU_HD:(ci + 1) * SGU_HD] + bias[:, lanes] for ci in range(nc)],
                axis=0))
        gated.append((u * jnp.concatenate(cols, axis=1)).astype(BF16))

    x1 = [x + _rms(_dot(a, w_out_ref[...]), g_post) for x, a in zip(xs, gated)]
    for sub, y in enumerate(_ffn_staged(x1, gf_pre, gf_post, w_up_ref, w_down_ref)):
        y_ref[pl.ds(sub * ts, ts), :] = y


def _resident(arr, layer=None):
    if layer is None:
        nd = arr.ndim
        return pl.BlockSpec(arr.shape, lambda *_: (0,) * nd, pipeline_mode=pl.Buffered(1))
    nd = arr.ndim - 1
    return pl.BlockSpec((None,) + arr.shape[1:], lambda *_: (layer,) + (0,) * nd,
                        pipeline_mode=pl.Buffered(1))


def _params(n_axes):
    return pltpu.CompilerParams(dimension_semantics=("arbitrary",) * n_axes,
                                vmem_limit_bytes=VMEM_LIMIT_BYTES)


def _split(weights):
    return [a for a, _ in weights], [_resident(a, l) for a, l in weights]


def _layer0_prompt(x, weights, layer, to_cast):
    b, s, d = x.shape
    tm = TILE_M_PROMPT
    nt = s // tm
    steps = b * nt
    arrs, specs = _split(weights)
    tile = pl.BlockSpec((None, tm, d), lambda i, j: (i, j, 0))
    pool = pl.BlockSpec((None, None, POOL_PREV, D_POOL), lambda i, j: (0, i, 0, 0))
    conv = pl.BlockSpec((None, None, CONV_W - 1, D_CONV), lambda i, j: (0, i, 0, 0))
    cast_in, cast_out, cast_shapes = [], [], []
    for w, slab in to_cast:
        _, rows, cols = w.shape
        assert rows % steps == 0
        cast_in.append(pl.BlockSpec((None, rows // steps, cols),
                                    lambda i, j, slab=slab: (slab, i * nt + j, 0)))
        cast_out.append(pl.BlockSpec((rows // steps, cols), lambda i, j: (i * nt + j, 0)))
        cast_shapes.append(jax.ShapeDtypeStruct((rows, cols), BF16))
    return pl.pallas_call(
        functools.partial(_layer0_prompt_kernel, layer=layer, n_cast=len(to_cast)),
        grid=(b, nt),
        in_specs=[tile] + specs + cast_in,
        out_specs=[tile, pool, conv] + cast_out,
        out_shape=[jax.ShapeDtypeStruct(x.shape, F32),
                   jax.ShapeDtypeStruct((1, b, POOL_PREV, D_POOL), F32),
                   jax.ShapeDtypeStruct((1, b, CONV_W - 1, D_CONV), F32)] + cast_shapes,
        scratch_shapes=[pltpu.VMEM((HALO, D_POOL), F32), pltpu.VMEM((HALO, D_CONV), F32),
                        pltpu.VMEM((D_POOL + D_CONV, d), BF16)],
        compiler_params=_params(2),
        name="layer0_prompt",
    )(x, *arrs, *[w for w, _ in to_cast])


def _layer0_sample(x, state_pool, state_conv, weights, layer, seq):
    n, d = x.shape
    tm = TILE_M_SAMPLE
    tb = tm // seq
    arrs, specs = _split(weights)
    tile = pl.BlockSpec((tm, d), lambda i: (i, 0))
    pool = pl.BlockSpec((None, tb, POOL_PREV, D_POOL), lambda i: (0, i, 0, 0))
    conv = pl.BlockSpec((None, tb, CONV_W - 1, D_CONV), lambda i: (0, i, 0, 0))
    return pl.pallas_call(
        functools.partial(_layer0_sample_kernel, layer=layer, seq=seq),
        grid=(n // tm,),
        in_specs=[tile, pool, conv] + specs,
        out_specs=[tile, pool, conv],
        out_shape=[jax.ShapeDtypeStruct(x.shape, F32),
                   jax.ShapeDtypeStruct(state_pool.shape, F32),
                   jax.ShapeDtypeStruct(state_conv.shape, F32)],
        scratch_shapes=[pltpu.VMEM((tb, HALO + seq, D_POOL), F32),
                        pltpu.VMEM((tb, HALO + seq, D_CONV), F32),
                        pltpu.VMEM((D_POOL + D_CONV, d), BF16)],
        compiler_params=_params(1),
        name="layer0_sample",
    )(x, state_pool, state_conv, *arrs)


def _layer1(x, weights, tm, layer, seq, emit_v, name):
    n, d = x.shape
    arrs, specs = _split(weights)
    tile = pl.BlockSpec((tm, d), lambda i: (i, 0))
    out_specs = [tile, tile] if emit_v else [tile]
    out_shape = [jax.ShapeDtypeStruct(x.shape, F32)] * len(out_specs)
    return pl.pallas_call(
        functools.partial(_layer1_kernel, layer=layer, seq=seq, emit_v=emit_v),
        grid=(n // tm,),
        in_specs=[tile] + specs,
        out_specs=out_specs,
        out_shape=out_shape,
        scratch_shapes=[pltpu.VMEM((SGU_HEADS, CHUNK, CHUNK), BF16),
                        pltpu.VMEM((CHUNK, D_SGU), F32)],
        compiler_params=_params(1),
        name=name,
    )(x, *arrs)


def kernel(x_prompt, x_sample, state_pool, state_conv, g_mix_pre, g_mix_post, g_ffn_pre,
           g_ffn_post, w_in_ab, w_pool_grp, pool_scale, conv_w, w_out_ab, w_uv, g_v, w_spatial,
           b_spatial, w_out_c, w_up, w_down):
    bp, sp, d = x_prompt.shape
    bs, ss, _ = x_sample.shape
    assert ss < CHUNK and CHUNK % ss == 0 and CONV_W - 1 <= ss and sp % CHUNK == 0

    gains = ((g_mix_pre, None), (g_mix_post, None), (g_ffn_pre, None), (g_ffn_post, None))
    w0 = gains + ((w_pool_grp, 0), (pool_scale, None), (conv_w, 0),
                  (w_in_ab[0].astype(BF16), None), (w_out_ab[0].astype(BF16), None),
                  (w_up[0].astype(BF16), None), (w_down[0].astype(BF16), None))

    y0, pool_p, conv_p, w_uv_b, w_out_c_b, w_up1_b, w_down1_b = _layer0_prompt(
        x_prompt, w0, 0, ((w_uv, 0), (w_out_c, 0), (w_up, 1), (w_down, 1)))
    w1 = gains + ((g_v, None), (w_spatial, 0), (b_spatial, 0),
                  (w_uv_b, None), (w_out_c_b, None), (w_up1_b, None), (w_down1_b, None))
    (y_prompt,) = _layer1(y0.reshape(bp * sp, d), w1, TILE_M_PROMPT, 1, None, False,
                          "layer1_prompt")

    ys0, pool_s, conv_s = _layer0_sample(x_sample.reshape(bs * ss, d), state_pool, state_conv,
                                         w0, 0, ss)
    ys1, v_s = _layer1(ys0, w1, TILE_M_SAMPLE, 1, ss, True, "layer1_sample")
    return (y_prompt.reshape(bp, sp, d), ys1.reshape(bs, ss, d), pool_p, pool_s, conv_p, conv_s,
            v_s.reshape(1, bs, ss, D_SGU))
```

```python
import functools

import jax
import jax.numpy as jnp
from jax import lax
from jax.experimental import pallas as pl
from jax.experimental.pallas import tpu as pltpu

D_MODEL = 1024
EPS = 1e-6
D_POOL = D_MODEL // 2
POOL_WINDOWS = (2, 4, 8, 16)
POOL_GC = D_POOL // len(POOL_WINDOWS)
POOL_PREV = max(POOL_WINDOWS) - 1
D_CONV = D_MODEL // 2
CONV_W = 3
D_SGU = D_MODEL
SGU_HEADS = 8
SGU_HD = D_SGU // SGU_HEADS
CHUNK = 128
PAST_LEN = 16384

HALO = 16
TILE_M_PROMPT = 512
TILE_M_SAMPLE = 512
SUB_ROWS = (1, 1)
SUB_TILES_SAMPLE0 = 1
STAGE_ROWS, STAGE_COLS = 512, 1024
VMEM_LIMIT_BYTES = 60 * 1024 * 1024

F32 = jnp.float32
BF16 = jnp.bfloat16


def _rms(x, g):
    return x * lax.rsqrt(jnp.mean(x * x, axis=-1, keepdims=True) + EPS) * g


def _dot(a, b):
    return jnp.dot(a, b, preferred_element_type=F32)


def _shift_rows(a, k):
    return pltpu.roll(a, k, axis=0)


def _window_sums(full_u):
    sums = []
    for g, w in enumerate(POOL_WINDOWS):
        s = full_u[:, g * POOL_GC:(g + 1) * POOL_GC]
        k = 1
        while k < w:
            s = s + _shift_rows(s, k)
            k *= 2
        sums.append(s)
    return sums


def _pool_diff(sums, u, pos):
    outs = []
    for g, w in enumerate(POOL_WINDOWS):
        inv_cnt = 1.0 / jnp.minimum(pos + 1.0, float(w))
        outs.append(sums[g] * inv_cnt - u[:, g * POOL_GC:(g + 1) * POOL_GC])
    return jnp.concatenate(outs, axis=1)


def _fold_pool_into_out(w_grp_ref, scale_ref, w_out_ref, w_eff_scr):
    scale = scale_ref[...]
    for g in range(len(POOL_WINDOWS)):
        rows = slice(g * POOL_GC, (g + 1) * POOL_GC)
        a = w_grp_ref[g] * scale[:, rows]
        b = w_out_ref[rows, :].astype(F32)
        w_eff_scr[rows, :] = jnp.dot(a, b, precision=lax.Precision.HIGHEST,
                                     preferred_element_type=F32).astype(BF16)
    w_eff_scr[D_POOL:, :] = w_out_ref[D_POOL:, :]


def _conv_rows(full_cx, conv_w):
    return (conv_w[0:1] * _shift_rows(full_cx, 2) + conv_w[1:2] * _shift_rows(full_cx, 1)
            + conv_w[2:3] * full_cx)


def _gains(gain_refs, layer):
    return [r[layer:layer + 1, :] for r in gain_refs]


def _in_proj(x, g_pre, w_in_ref):
    p = _dot(_rms(x, g_pre).astype(BF16), w_in_ref[...])
    u = p[:, :D_POOL]
    cx = p[:, D_POOL + 2 * D_CONV:] * p[:, D_POOL:D_POOL + D_CONV]
    gate_b = p[:, D_POOL + D_CONV:D_POOL + 2 * D_CONV]
    return u, cx, gate_b


def _ffn(x, g_pre, g_post, w_up_ref, w_down_ref):
    h = _rms(x, g_pre).astype(BF16)
    a = jnp.maximum(_dot(h, w_up_ref[...]), 0.0)
    a = (a * a).astype(BF16)
    f = _dot(a, w_down_ref[...])
    return x + _rms(f, g_post)


def _sub_spans(tm):
    unit = tm // sum(SUB_ROWS)
    spans, r0 = [], 0
    for part in SUB_ROWS:
        spans.append((r0, part * unit))
        r0 += part * unit
    return spans


def _ffn_staged(xs, g_pre, g_post, w_up_ref, w_down_ref):
    acts = []
    for x in xs:
        a = jnp.maximum(_dot(_rms(x, g_pre).astype(BF16), w_up_ref[...]), 0.0)
        acts.append((a * a).astype(BF16))
    return [x + _rms(_dot(a, w_down_ref[...]), g_post) for x, a in zip(xs, acts)]


def _stage_chunks(shape):
    rows, cols = shape
    assert rows % STAGE_ROWS == 0 and cols % STAGE_COLS == 0
    return [(r, c) for r in range(0, rows, STAGE_ROWS) for c in range(0, cols, STAGE_COLS)]


def _stream_cast(srcs, slab, dsts, stage, sem):
    plan = [(src, dst, r0, c0) for src, dst in zip(srcs, dsts) for r0, c0 in _stage_chunks(dst.shape)]

    def fetch(k):
        src, _, r0, c0 = plan[k]
        return pltpu.make_async_copy(
            src.at[slab, pl.ds(r0, STAGE_ROWS), pl.ds(c0, STAGE_COLS)], stage.at[k % 2],
            sem.at[k % 2])

    fetch(0).start()
    for k, (_, dst, r0, c0) in enumerate(plan):
        if k + 1 < len(plan):
            fetch(k + 1).start()
        fetch(k).wait()
        dst[pl.ds(r0, STAGE_ROWS), pl.ds(c0, STAGE_COLS)] = stage[k % 2].astype(BF16)


def _export_copies(bufs, outs, sem):
    return [pltpu.make_async_copy(buf, out, sem.at[j]) for j, (buf, out) in enumerate(zip(bufs, outs))]


def _layer0_prompt_kernel(x_ref, g0, g1, g2, g3, w_grp_ref, scale_ref, conv_w_ref,
                          *rest, layer, n_cast):
    w_hbm, cast_in, rest = rest[:4], rest[4:4 + n_cast], rest[4 + n_cast:]
    (y_ref, pool_out_ref, conv_out_ref), w_exp, rest = rest[:3], rest[3:7], rest[7:]
    cast_out, rest = rest[:n_cast], rest[n_cast:]
    halo_u, halo_cx, w_eff_scr, w_in_ref, w_out_ref, w_up_ref, w_down_ref, stage, sem_in, sem_out = rest
    w_bufs = (w_in_ref, w_out_ref, w_up_ref, w_down_ref)
    tm = x_ref.shape[0]
    t = pl.program_id(1)
    first = (pl.program_id(0) == 0) & (t == 0)
    last = (pl.program_id(0) == pl.num_programs(0) - 1) & (t == pl.num_programs(1) - 1)
    g_pre, g_post, gf_pre, gf_post = _gains((g0, g1, g2, g3), layer)

    for src, dst in zip(cast_in, cast_out):
        dst[...] = src[...].astype(BF16)

    @pl.when(first)
    def _():
        _stream_cast(w_hbm, layer // 2, w_bufs, stage, sem_in)
        for c in _export_copies(w_bufs, w_exp, sem_out):
            c.start()
        _fold_pool_into_out(w_grp_ref, scale_ref, w_out_ref, w_eff_scr)

    @pl.when(last)
    def _():
        for c in _export_copies(w_bufs, w_exp, sem_out):
            c.wait()

    @pl.when(t == 0)
    def _():
        halo_u[...] = jnp.zeros_like(halo_u)
        halo_cx[...] = jnp.zeros_like(halo_cx)

    spans = _sub_spans(tm)
    xs = [x_ref[pl.ds(r0, ts), :] for r0, ts in spans]
    proj = [_in_proj(x, g_pre, w_in_ref) for x in xs]

    hist_u, hist_cx = halo_u[...], halo_cx[...]
    yabs = []
    for (r0, ts), (u, cx, gate_b) in zip(spans, proj):
        full_u = jnp.concatenate([hist_u, u], axis=0)
        full_cx = jnp.concatenate([hist_cx, cx], axis=0)
        hist_u, hist_cx = u[ts - HALO:], cx[ts - HALO:]
        pos = (lax.broadcasted_iota(jnp.int32, (ts, 1), 0) + (t * tm + r0)).astype(F32)
        sums = [s[HALO:] for s in _window_sums(full_u)]
        d = _pool_diff(sums, u, pos)
        cz = _conv_rows(full_cx, conv_w_ref[...])[HALO:]
        yabs.append(jnp.concatenate([d, gate_b * cz], axis=1).astype(BF16))
    halo_u[...] = hist_u
    halo_cx[...] = hist_cx

    x1 = [x + _rms(_dot(yab, w_eff_scr[...]), g_post) for x, yab in zip(xs, yabs)]
    for (r0, ts), y in zip(spans, _ffn_staged(x1, gf_pre, gf_post, w_up_ref, w_down_ref)):
        y_ref[pl.ds(r0, ts), :] = y

    pool_out_ref[...] = halo_u[HALO - POOL_PREV:, :]
    conv_out_ref[...] = halo_cx[HALO - (CONV_W - 1):, :]


def _layer0_sample_kernel(x_ref, pool_ref, conv_ref, g0, g1, g2, g3, w_grp_ref, scale_ref,
                          conv_w_ref, w_in_ref, w_out_ref, w_up_ref, w_down_ref,
                          y_ref, pool_out_ref, conv_out_ref,
                          full_u_scr, full_cx_scr, w_eff_scr, *, layer, seq):
    tm = x_ref.shape[0]
    tb = tm // seq
    grp = HALO + seq
    g_pre, g_post, gf_pre, gf_post = _gains((g0, g1, g2, g3), layer)

    @pl.when(pl.program_id(0) == 0)
    def _():
        _fold_pool_into_out(w_grp_ref, scale_ref, w_out_ref, w_eff_scr)

    ts = tm // SUB_TILES_SAMPLE0
    tbs = tb // SUB_TILES_SAMPLE0
    subs = range(SUB_TILES_SAMPLE0)
    xs = [x_ref[pl.ds(sub * ts, ts), :] for sub in subs]
    proj = [_in_proj(x, g_pre, w_in_ref) for x in xs]

    def new_rows(a):
        return a.reshape(tbs, grp, a.shape[-1])[:, HALO:].reshape(ts, a.shape[-1])

    pos = (lax.broadcasted_iota(jnp.int32, (tbs, seq, 1), 1).reshape(ts, 1) + PAST_LEN).astype(F32)
    yabs = []
    for sub, (u, cx, gate_b) in enumerate(proj):
        sq = pl.ds(sub * tbs, tbs)
        full_u_scr[sq, :HALO - POOL_PREV, :] = jnp.zeros((tbs, HALO - POOL_PREV, D_POOL), F32)
        for r in range(POOL_PREV):
            full_u_scr[sq, HALO - POOL_PREV + r, :] = pool_ref[r, sq, :]
        full_u_scr[sq, HALO:, :] = u.reshape(tbs, seq, D_POOL)
        full_cx_scr[sq, :HALO - (CONV_W - 1), :] = jnp.zeros((tbs, HALO - (CONV_W - 1), D_CONV), F32)
        full_cx_scr[sq, HALO - (CONV_W - 1):HALO, :] = conv_ref[sq]
        full_cx_scr[sq, HALO:, :] = cx.reshape(tbs, seq, D_CONV)
        for r in range(POOL_PREV):
            pool_out_ref[r, sq, :] = full_u_scr[sq, grp - POOL_PREV + r, :]
        conv_out_ref[sq] = full_cx_scr[sq, grp - (CONV_W - 1):, :]
        full_u = full_u_scr[sq].reshape(tbs * grp, D_POOL)
        full_cx = full_cx_scr[sq].reshape(tbs * grp, D_CONV)
        sums = [new_rows(s) for s in _window_sums(full_u)]
        d = _pool_diff(sums, u, pos)
        cz = new_rows(_conv_rows(full_cx, conv_w_ref[...]))
        yabs.append(jnp.concatenate([d, gate_b * cz], axis=1).astype(BF16))

    x1 = [x + _rms(_dot(yab, w_eff_scr[...]), g_post) for x, yab in zip(xs, yabs)]
    for sub, y in enumerate(_ffn_staged(x1, gf_pre, gf_post, w_up_ref, w_down_ref)):
        y_ref[pl.ds(sub * ts, ts), :] = y


def _sgu_tables(w_sp_ref, b_sp_ref, wm_scr, bias_scr, seq):
    r = lax.broadcasted_iota(jnp.int32, (CHUNK, CHUNK), 0)
    c = lax.broadcasted_iota(jnp.int32, (CHUNK, CHUNK), 1)
    keep = r >= c
    b = b_sp_ref[...]
    if seq is not None:
        keep = keep & (r // seq == c // seq)
        pick = (r % seq == c).astype(BF16)
        pick_t = (c % seq == r).astype(BF16)
        b = jnp.where(lax.broadcasted_iota(jnp.int32, b.shape, 1) < seq, b, 0.0)
        k = seq
        while k < CHUNK:
            b = b + pltpu.roll(b, k, axis=1)
            k *= 2
    mask = keep.astype(F32)
    for hh in range(SGU_HEADS):
        w = w_sp_ref[hh]
        if seq is not None:
            w = _dot(_dot(pick, w.astype(BF16)).astype(BF16), pick_t)
        wm_scr[hh] = (w * mask).astype(BF16)
        rows = jnp.broadcast_to(b[hh:hh + 1, :], (CHUNK, SGU_HD))
        bias_scr[:, hh * SGU_HD:(hh + 1) * SGU_HD] = rows.T


def _layer1_kernel(x_ref, g0, g1, g2, g3, g_v_ref, w_sp_ref, b_sp_ref,
                   w_uv_ref, w_out_ref, w_up_ref, w_down_ref, y_ref, *rest, layer, seq, emit_v):
    if emit_v:
        v_ref, wm_scr, bias_scr = rest
    else:
        wm_scr, bias_scr = rest
    tm = x_ref.shape[0]
    g_pre, g_post, gf_pre, gf_post = _gains((g0, g1, g2, g3), layer)

    @pl.when(pl.program_id(0) == 0)
    def _():
        _sgu_tables(w_sp_ref, b_sp_ref, wm_scr, bias_scr, seq)

    spans = _sub_spans(tm)
    xs = [x_ref[pl.ds(r0, ts), :] for r0, ts in spans]
    ps = [_dot(_rms(x, g_pre).astype(BF16), w_uv_ref[...]) for x in xs]

    bias = bias_scr[...]
    gated = []
    for (r0, ts), p in zip(spans, ps):
        nc = ts // CHUNK
        u = p[:, :D_SGU]
        v = _rms(p[:, D_SGU:], g_v_ref[...])
        if emit_v:
            v_ref[pl.ds(r0, ts), :] = v
        vb = v.astype(BF16)
        cols = []
        for hh in range(SGU_HEADS):
            lanes = slice(hh * SGU_HD, (hh + 1) * SGU_HD)
            rhs = jnp.concatenate([vb[ci * CHUNK:(ci + 1) * CHUNK, lanes] for ci in range(nc)],
                                  axis=1)
            out = _dot(wm_scr[hh], rhs)
            cols.append(jnp.concatenate(
                [out[:, ci * SGU_HD:(ci + 1) * SGU_HD] + bias[:, lanes] for ci in range(nc)],
                axis=0))
        gated.append((u * jnp.concatenate(cols, axis=1)).astype(BF16))

    x1 = [x + _rms(_dot(a, w_out_ref[...]), g_post) for x, a in zip(xs, gated)]
    for (r0, ts), y in zip(spans, _ffn_staged(x1, gf_pre, gf_post, w_up_ref, w_down_ref)):
        y_ref[pl.ds(r0, ts), :] = y


def _resident(arr, layer=None):
    if layer is None:
        nd = arr.ndim
        return pl.BlockSpec(arr.shape, lambda *_: (0,) * nd, pipeline_mode=pl.Buffered(1))
    nd = arr.ndim - 1
    return pl.BlockSpec((None,) + arr.shape[1:], lambda *_: (layer,) + (0,) * nd,
                        pipeline_mode=pl.Buffered(1))


def _params(n_axes):
    return pltpu.CompilerParams(dimension_semantics=("arbitrary",) * n_axes,
                                vmem_limit_bytes=VMEM_LIMIT_BYTES)


def _split(weights):
    return [a for a, _ in weights], [_resident(a, l) for a, l in weights]


def _layer0_prompt(x, weights, big, layer, to_cast):
    b, s, d = x.shape
    tm = TILE_M_PROMPT
    nt = s // tm
    steps = b * nt
    arrs, specs = _split(weights)
    any_spec = pl.BlockSpec(memory_space=pl.ANY)
    big_shapes = [w.shape[1:] for w in big]
    tile = pl.BlockSpec((None, tm, d), lambda i, j: (i, j, 0))
    pool = pl.BlockSpec((None, None, POOL_PREV, D_POOL), lambda i, j: (0, i, 0, 0))
    conv = pl.BlockSpec((None, None, CONV_W - 1, D_CONV), lambda i, j: (0, i, 0, 0))
    cast_in, cast_out, cast_shapes = [], [], []
    for w, slab in to_cast:
        _, rows, cols = w.shape
        assert rows % steps == 0
        cast_in.append(pl.BlockSpec((None, rows // steps, cols),
                                    lambda i, j, slab=slab: (slab, i * nt + j, 0)))
        cast_out.append(pl.BlockSpec((rows // steps, cols), lambda i, j: (i * nt + j, 0)))
        cast_shapes.append(jax.ShapeDtypeStruct((rows, cols), BF16))
    return pl.pallas_call(
        functools.partial(_layer0_prompt_kernel, layer=layer, n_cast=len(to_cast)),
        grid=(b, nt),
        in_specs=[tile] + specs + [any_spec] * len(big) + cast_in,
        out_specs=[tile, pool, conv] + [any_spec] * len(big) + cast_out,
        out_shape=[jax.ShapeDtypeStruct(x.shape, F32),
                   jax.ShapeDtypeStruct((1, b, POOL_PREV, D_POOL), F32),
                   jax.ShapeDtypeStruct((1, b, CONV_W - 1, D_CONV), F32)]
        + [jax.ShapeDtypeStruct(sh, BF16) for sh in big_shapes] + cast_shapes,
        scratch_shapes=[pltpu.VMEM((HALO, D_POOL), F32), pltpu.VMEM((HALO, D_CONV), F32),
                        pltpu.VMEM((D_POOL + D_CONV, d), BF16)]
        + [pltpu.VMEM(sh, BF16) for sh in big_shapes]
        + [pltpu.VMEM((2, STAGE_ROWS, STAGE_COLS), F32), pltpu.SemaphoreType.DMA((2,)),
           pltpu.SemaphoreType.DMA((len(big),))],
        compiler_params=_params(2),
        name="layer0_prompt",
    )(x, *arrs, *big, *[w for w, _ in to_cast])


def _layer0_sample(x, state_pool, state_conv, weights, layer, seq):
    n, d = x.shape
    tm = TILE_M_SAMPLE
    tb = tm // seq
    arrs, specs = _split(weights)
    tile = pl.BlockSpec((tm, d), lambda i: (i, 0))
    pool_t = jnp.swapaxes(state_pool, 1, 2)
    pool = pl.BlockSpec((None, POOL_PREV, tb, D_POOL), lambda i: (0, 0, i, 0))
    conv = pl.BlockSpec((None, tb, CONV_W - 1, D_CONV), lambda i: (0, i, 0, 0))
    y, pool_new, conv_new = pl.pallas_call(
        functools.partial(_layer0_sample_kernel, layer=layer, seq=seq),
        grid=(n // tm,),
        in_specs=[tile, pool, conv] + specs,
        out_specs=[tile, pool, conv],
        out_shape=[jax.ShapeDtypeStruct(x.shape, F32),
                   jax.ShapeDtypeStruct(pool_t.shape, F32),
                   jax.ShapeDtypeStruct(state_conv.shape, F32)],
        scratch_shapes=[pltpu.VMEM((tb, HALO + seq, D_POOL), F32),
                        pltpu.VMEM((tb, HALO + seq, D_CONV), F32),
                        pltpu.VMEM((D_POOL + D_CONV, d), BF16)],
        compiler_params=_params(1),
        name="layer0_sample",
    )(x, pool_t, state_conv, *arrs)
    return y, jnp.swapaxes(pool_new, 1, 2), conv_new


def _layer1(x, weights, tm, layer, seq, emit_v, name):
    n, d = x.shape
    arrs, specs = _split(weights)
    tile = pl.BlockSpec((tm, d), lambda i: (i, 0))
    out_specs = [tile, tile] if emit_v else [tile]
    out_shape = [jax.ShapeDtypeStruct(x.shape, F32)] * len(out_specs)
    return pl.pallas_call(
        functools.partial(_layer1_kernel, layer=layer, seq=seq, emit_v=emit_v),
        grid=(n // tm,),
        in_specs=[tile] + specs,
        out_specs=out_specs,
        out_shape=out_shape,
        scratch_shapes=[pltpu.VMEM((SGU_HEADS, CHUNK, CHUNK), BF16),
                        pltpu.VMEM((CHUNK, D_SGU), F32)],
        compiler_params=_params(1),
        name=name,
    )(x, *arrs)


def kernel(x_prompt, x_sample, state_pool, state_conv, g_mix_pre, g_mix_post, g_ffn_pre,
           g_ffn_post, w_in_ab, w_pool_grp, pool_scale, conv_w, w_out_ab, w_uv, g_v, w_spatial,
           b_spatial, w_out_c, w_up, w_down):
    bp, sp, d = x_prompt.shape
    bs, ss, _ = x_sample.shape
    assert ss < CHUNK and CHUNK % ss == 0 and CONV_W - 1 <= ss and sp % CHUNK == 0

    gains = ((g_mix_pre, None), (g_mix_post, None), (g_ffn_pre, None), (g_ffn_post, None))
    w0_small = gains + ((w_pool_grp, 0), (pool_scale, None), (conv_w, 0))

    (y0, pool_p, conv_p, w_in_b, w_out_b, w_up0_b, w_down0_b,
     w_uv_b, w_out_c_b, w_up1_b, w_down1_b) = _layer0_prompt(
        x_prompt, w0_small, (w_in_ab, w_out_ab, w_up, w_down), 0,
        ((w_uv, 0), (w_out_c, 0), (w_up, 1), (w_down, 1)))
    w0 = w0_small + ((w_in_b, None), (w_out_b, None), (w_up0_b, None), (w_down0_b, None))
    w1 = gains + ((g_v, None), (w_spatial, 0), (b_spatial, 0),
                  (w_uv_b, None), (w_out_c_b, None), (w_up1_b, None), (w_down1_b, None))
    (y_prompt,) = _layer1(y0.reshape(bp * sp, d), w1, TILE_M_PROMPT, 1, None, False,
                          "layer1_prompt")

    ys0, pool_s, conv_s = _layer0_sample(x_sample.reshape(bs * ss, d), state_pool, state_conv,
                                         w0, 0, ss)
    ys1, v_s = _layer1(ys0, w1, TILE_M_SAMPLE, 1, ss, True, "layer1_sample")
    return (y_prompt.reshape(bp, sp, d), ys1.reshape(bs, ss, d), pool_p, pool_s, conv_p, conv_s,
            v_s.reshape(1, bs, ss, D_SGU))
```

```python
import functools

import jax
import jax.numpy as jnp
from jax import lax
from jax.experimental import pallas as pl
from jax.experimental.pallas import tpu as pltpu

D_MODEL = 1024
EPS = 1e-6
D_POOL = D_MODEL // 2
POOL_WINDOWS = (2, 4, 8, 16)
POOL_GC = D_POOL // len(POOL_WINDOWS)
POOL_PREV = max(POOL_WINDOWS) - 1
D_CONV = D_MODEL // 2
CONV_W = 3
D_SGU = D_MODEL
SGU_HEADS = 8
SGU_HD = D_SGU // SGU_HEADS
CHUNK = 128
PAST_LEN = 16384

HALO = 16
TILE_M_PROMPT = 512
TILE_M_PROMPT_L1 = 1024
TILE_M_SAMPLE = 512
SUB_TILE_ROWS = 256
SUB_GROUP = 2
SUB_TILES_SAMPLE0 = 1
STAGE_ROWS, STAGE_COLS = 512, 1024
VMEM_LIMIT_BYTES = 60 * 1024 * 1024

F32 = jnp.float32
BF16 = jnp.bfloat16


def _rms(x, g):
    return x * lax.rsqrt(jnp.mean(x * x, axis=-1, keepdims=True) + EPS) * g


def _dot(a, b):
    return jnp.dot(a, b, preferred_element_type=F32)


def _shift_rows(a, k):
    return pltpu.roll(a, k, axis=0)


def _window_sums(full_u):
    sums = []
    for g, w in enumerate(POOL_WINDOWS):
        s = full_u[:, g * POOL_GC:(g + 1) * POOL_GC]
        k = 1
        while k < w:
            s = s + _shift_rows(s, k)
            k *= 2
        sums.append(s)
    return sums


def _pool_diff(sums, u, pos):
    outs = []
    for g, w in enumerate(POOL_WINDOWS):
        inv_cnt = 1.0 / jnp.minimum(pos + 1.0, float(w))
        outs.append(sums[g] * inv_cnt - u[:, g * POOL_GC:(g + 1) * POOL_GC])
    return jnp.concatenate(outs, axis=1)


def _fold_pool_into_out(w_grp_ref, scale_ref, w_out_ref, w_eff_scr):
    scale = scale_ref[...]
    for g in range(len(POOL_WINDOWS)):
        rows = slice(g * POOL_GC, (g + 1) * POOL_GC)
        a = w_grp_ref[g] * scale[:, rows]
        b = w_out_ref[rows, :].astype(F32)
        w_eff_scr[rows, :] = jnp.dot(a, b, precision=lax.Precision.HIGHEST,
                                     preferred_element_type=F32).astype(BF16)
    w_eff_scr[D_POOL:, :] = w_out_ref[D_POOL:, :]


def _conv_rows(full_cx, conv_w):
    return (conv_w[0:1] * _shift_rows(full_cx, 2) + conv_w[1:2] * _shift_rows(full_cx, 1)
            + conv_w[2:3] * full_cx)


def _gains(gain_refs, layer):
    return [r[layer:layer + 1, :] for r in gain_refs]


def _in_proj(x, g_pre, w_in_ref):
    p = _dot(_rms(x, g_pre).astype(BF16), w_in_ref[...])
    u = p[:, :D_POOL]
    cx = p[:, D_POOL + 2 * D_CONV:] * p[:, D_POOL:D_POOL + D_CONV]
    gate_b = p[:, D_POOL + D_CONV:D_POOL + 2 * D_CONV]
    return u, cx, gate_b


def _ffn(x, g_pre, g_post, w_up_ref, w_down_ref):
    h = _rms(x, g_pre).astype(BF16)
    a = jnp.maximum(_dot(h, w_up_ref[...]), 0.0)
    a = (a * a).astype(BF16)
    f = _dot(a, w_down_ref[...])
    return x + _rms(f, g_post)


def _sub_spans(tm):
    assert tm % SUB_TILE_ROWS == 0
    return [(r0, SUB_TILE_ROWS) for r0 in range(0, tm, SUB_TILE_ROWS)]


def _span_groups(tm):
    spans = _sub_spans(tm)
    return [spans[k:k + SUB_GROUP] for k in range(0, len(spans), SUB_GROUP)]


def _ffn_staged(xs, g_pre, g_post, w_up_ref, w_down_ref):
    acts = []
    for x in xs:
        a = jnp.maximum(_dot(_rms(x, g_pre).astype(BF16), w_up_ref[...]), 0.0)
        acts.append((a * a).astype(BF16))
    return [x + _rms(_dot(a, w_down_ref[...]), g_post) for x, a in zip(xs, acts)]


def _stage_chunks(shape):
    rows, cols = shape
    assert rows % STAGE_ROWS == 0 and cols % STAGE_COLS == 0
    return [(r, c) for r in range(0, rows, STAGE_ROWS) for c in range(0, cols, STAGE_COLS)]


def _stream_cast(srcs, slab, dsts, stage, sem):
    plan = [(src, dst, r0, c0) for src, dst in zip(srcs, dsts) for r0, c0 in _stage_chunks(dst.shape)]

    def fetch(k):
        src, _, r0, c0 = plan[k]
        return pltpu.make_async_copy(
            src.at[slab, pl.ds(r0, STAGE_ROWS), pl.ds(c0, STAGE_COLS)], stage.at[k % 2],
            sem.at[k % 2])

    fetch(0).start()
    for k, (_, dst, r0, c0) in enumerate(plan):
        if k + 1 < len(plan):
            fetch(k + 1).start()
        fetch(k).wait()
        dst[pl.ds(r0, STAGE_ROWS), pl.ds(c0, STAGE_COLS)] = stage[k % 2].astype(BF16)


def _export_copies(bufs, outs, sem):
    return [pltpu.make_async_copy(buf, out, sem.at[j]) for j, (buf, out) in enumerate(zip(bufs, outs))]


def _layer0_prompt_kernel(x_ref, g0, g1, g2, g3, w_grp_ref, scale_ref, conv_w_ref,
                          *rest, layer, n_cast):
    w_hbm, cast_in, rest = rest[:4], rest[4:4 + n_cast], rest[4 + n_cast:]
    (y_ref, pool_out_ref, conv_out_ref), w_exp, rest = rest[:3], rest[3:7], rest[7:]
    cast_out, rest = rest[:n_cast], rest[n_cast:]
    halo_u, halo_cx, w_eff_scr, w_in_ref, w_out_ref, w_up_ref, w_down_ref, stage, sem_in, sem_out = rest
    w_bufs = (w_in_ref, w_out_ref, w_up_ref, w_down_ref)
    tm = x_ref.shape[0]
    t = pl.program_id(1)
    first = (pl.program_id(0) == 0) & (t == 0)
    last = (pl.program_id(0) == pl.num_programs(0) - 1) & (t == pl.num_programs(1) - 1)
    g_pre, g_post, gf_pre, gf_post = _gains((g0, g1, g2, g3), layer)

    for src, dst in zip(cast_in, cast_out):
        dst[...] = src[...].astype(BF16)

    @pl.when(first)
    def _():
        _stream_cast(w_hbm, layer // 2, w_bufs, stage, sem_in)
        for c in _export_copies(w_bufs, w_exp, sem_out):
            c.start()
        _fold_pool_into_out(w_grp_ref, scale_ref, w_out_ref, w_eff_scr)

    @pl.when(last)
    def _():
        for c in _export_copies(w_bufs, w_exp, sem_out):
            c.wait()

    @pl.when(t == 0)
    def _():
        halo_u[...] = jnp.zeros_like(halo_u)
        halo_cx[...] = jnp.zeros_like(halo_cx)

    hist_u, hist_cx = halo_u[...], halo_cx[...]
    for spans in _span_groups(tm):
        xs = [x_ref[pl.ds(r0, ts), :] for r0, ts in spans]
        proj = [_in_proj(x, g_pre, w_in_ref) for x in xs]

        yabs = []
        for (r0, ts), (u, cx, gate_b) in zip(spans, proj):
            full_u = jnp.concatenate([hist_u, u], axis=0)
            full_cx = jnp.concatenate([hist_cx, cx], axis=0)
            hist_u, hist_cx = u[ts - HALO:], cx[ts - HALO:]
            pos = (lax.broadcasted_iota(jnp.int32, (ts, 1), 0) + (t * tm + r0)).astype(F32)
            sums = [s[HALO:] for s in _window_sums(full_u)]
            d = _pool_diff(sums, u, pos)
            cz = _conv_rows(full_cx, conv_w_ref[...])[HALO:]
            yabs.append(jnp.concatenate([d, gate_b * cz], axis=1).astype(BF16))

        x1 = [x + _rms(_dot(yab, w_eff_scr[...]), g_post) for x, yab in zip(xs, yabs)]
        for (r0, ts), y in zip(spans, _ffn_staged(x1, gf_pre, gf_post, w_up_ref, w_down_ref)):
            y_ref[pl.ds(r0, ts), :] = y
    halo_u[...] = hist_u
    halo_cx[...] = hist_cx

    pool_out_ref[...] = halo_u[HALO - POOL_PREV:, :]
    conv_out_ref[...] = halo_cx[HALO - (CONV_W - 1):, :]


def _layer0_sample_kernel(x_ref, pool_ref, conv_ref, g0, g1, g2, g3, w_grp_ref, scale_ref,
                          conv_w_ref, w_in_ref, w_out_ref, w_up_ref, w_down_ref,
                          y_ref, pool_out_ref, conv_out_ref,
                          full_u_scr, full_cx_scr, w_eff_scr, *, layer, seq):
    tm = x_ref.shape[0]
    tb = tm // seq
    grp = HALO + seq
    g_pre, g_post, gf_pre, gf_post = _gains((g0, g1, g2, g3), layer)

    @pl.when(pl.program_id(0) == 0)
    def _():
        _fold_pool_into_out(w_grp_ref, scale_ref, w_out_ref, w_eff_scr)

    ts = tm // SUB_TILES_SAMPLE0
    tbs = tb // SUB_TILES_SAMPLE0
    subs = range(SUB_TILES_SAMPLE0)
    xs = [x_ref[pl.ds(sub * ts, ts), :] for sub in subs]
    proj = [_in_proj(x, g_pre, w_in_ref) for x in xs]

    def new_rows(a):
        return a.reshape(tbs, grp, a.shape[-1])[:, HALO:].reshape(ts, a.shape[-1])

    pos = (lax.broadcasted_iota(jnp.int32, (tbs, seq, 1), 1).reshape(ts, 1) + PAST_LEN).astype(F32)
    yabs = []
    for sub, (u, cx, gate_b) in enumerate(proj):
        sq = pl.ds(sub * tbs, tbs)
        full_u_scr[sq, :HALO - POOL_PREV, :] = jnp.zeros((tbs, HALO - POOL_PREV, D_POOL), F32)
        for r in range(POOL_PREV):
            full_u_scr[sq, HALO - POOL_PREV + r, :] = pool_ref[r, sq, :]
        full_u_scr[sq, HALO:, :] = u.reshape(tbs, seq, D_POOL)
        full_cx_scr[sq, :HALO - (CONV_W - 1), :] = jnp.zeros((tbs, HALO - (CONV_W - 1), D_CONV), F32)
        full_cx_scr[sq, HALO - (CONV_W - 1):HALO, :] = conv_ref[sq]
        full_cx_scr[sq, HALO:, :] = cx.reshape(tbs, seq, D_CONV)
        for r in range(POOL_PREV):
            pool_out_ref[r, sq, :] = full_u_scr[sq, grp - POOL_PREV + r, :]
        conv_out_ref[sq] = full_cx_scr[sq, grp - (CONV_W - 1):, :]
        full_u = full_u_scr[sq].reshape(tbs * grp, D_POOL)
        full_cx = full_cx_scr[sq].reshape(tbs * grp, D_CONV)
        sums = [new_rows(s) for s in _window_sums(full_u)]
        d = _pool_diff(sums, u, pos)
        cz = new_rows(_conv_rows(full_cx, conv_w_ref[...]))
        yabs.append(jnp.concatenate([d, gate_b * cz], axis=1).astype(BF16))

    x1 = [x + _rms(_dot(yab, w_eff_scr[...]), g_post) for x, yab in zip(xs, yabs)]
    for sub, y in enumerate(_ffn_staged(x1, gf_pre, gf_post, w_up_ref, w_down_ref)):
        y_ref[pl.ds(sub * ts, ts), :] = y


def _sgu_tables(w_sp_ref, b_sp_ref, wm_scr, bias_scr, seq):
    r = lax.broadcasted_iota(jnp.int32, (CHUNK, CHUNK), 0)
    c = lax.broadcasted_iota(jnp.int32, (CHUNK, CHUNK), 1)
    keep = r >= c
    b = b_sp_ref[...]
    if seq is not None:
        keep = keep & (r // seq == c // seq)
        pick = (r % seq == c).astype(BF16)
        pick_t = (c % seq == r).astype(BF16)
        b = jnp.where(lax.broadcasted_iota(jnp.int32, b.shape, 1) < seq, b, 0.0)
        k = seq
        while k < CHUNK:
            b = b + pltpu.roll(b, k, axis=1)
            k *= 2
    mask = keep.astype(F32)
    for hh in range(SGU_HEADS):
        w = w_sp_ref[hh]
        if seq is not None:
            w = _dot(_dot(pick, w.astype(BF16)).astype(BF16), pick_t)
        wm_scr[hh] = (w * mask).astype(BF16)
        rows = jnp.broadcast_to(b[hh:hh + 1, :], (CHUNK, SGU_HD))
        bias_scr[:, hh * SGU_HD:(hh + 1) * SGU_HD] = rows.T


def _layer1_kernel(x_ref, g0, g1, g2, g3, g_v_ref, w_sp_ref, b_sp_ref,
                   w_uv_ref, w_out_ref, w_up_ref, w_down_ref, y_ref, *rest, layer, seq, emit_v):
    if emit_v:
        v_ref, wm_scr, bias_scr = rest
    else:
        wm_scr, bias_scr = rest
    tm = x_ref.shape[0]
    g_pre, g_post, gf_pre, gf_post = _gains((g0, g1, g2, g3), layer)

    @pl.when(pl.program_id(0) == 0)
    def _():
        _sgu_tables(w_sp_ref, b_sp_ref, wm_scr, bias_scr, seq)

    bias = bias_scr[...]
    for spans in _span_groups(tm):
        xs = [x_ref[pl.ds(r0, ts), :] for r0, ts in spans]
        ps = [_dot(_rms(x, g_pre).astype(BF16), w_uv_ref[...]) for x in xs]

        gated = []
        for (r0, ts), p in zip(spans, ps):
            nc = ts // CHUNK
            u = p[:, :D_SGU]
            v = _rms(p[:, D_SGU:], g_v_ref[...])
            if emit_v:
                v_ref[pl.ds(r0, ts), :] = v
            vb = v.astype(BF16)
            cols = []
            for hh in range(SGU_HEADS):
                lanes = slice(hh * SGU_HD, (hh + 1) * SGU_HD)
                rhs = jnp.concatenate(
                    [vb[ci * CHUNK:(ci + 1) * CHUNK, lanes] for ci in range(nc)], axis=1)
                out = _dot(wm_scr[hh], rhs)
                cols.append(jnp.concatenate(
                    [out[:, ci * SGU_HD:(ci + 1) * SGU_HD] + bias[:, lanes] for ci in range(nc)],
                    axis=0))
            gated.append((u * jnp.concatenate(cols, axis=1)).astype(BF16))

        x1 = [x + _rms(_dot(a, w_out_ref[...]), g_post) for x, a in zip(xs, gated)]
        for (r0, ts), y in zip(spans, _ffn_staged(x1, gf_pre, gf_post, w_up_ref, w_down_ref)):
            y_ref[pl.ds(r0, ts), :] = y


def _resident(arr, layer=None):
    if layer is None:
        nd = arr.ndim
        return pl.BlockSpec(arr.shape, lambda *_: (0,) * nd, pipeline_mode=pl.Buffered(1))
    nd = arr.ndim - 1
    return pl.BlockSpec((None,) + arr.shape[1:], lambda *_: (layer,) + (0,) * nd,
                        pipeline_mode=pl.Buffered(1))


def _params(n_axes):
    return pltpu.CompilerParams(dimension_semantics=("arbitrary",) * n_axes,
                                vmem_limit_bytes=VMEM_LIMIT_BYTES)


def _split(weights):
    return [a for a, _ in weights], [_resident(a, l) for a, l in weights]


def _layer0_prompt(x, weights, big, layer, to_cast):
    b, s, d = x.shape
    tm = TILE_M_PROMPT
    nt = s // tm
    steps = b * nt
    arrs, specs = _split(weights)
    any_spec = pl.BlockSpec(memory_space=pl.ANY)
    big_shapes = [w.shape[1:] for w in big]
    tile = pl.BlockSpec((None, tm, d), lambda i, j: (i, j, 0))
    pool = pl.BlockSpec((None, None, POOL_PREV, D_POOL), lambda i, j: (0, i, 0, 0))
    conv = pl.BlockSpec((None, None, CONV_W - 1, D_CONV), lambda i, j: (0, i, 0, 0))
    cast_in, cast_out, cast_shapes = [], [], []
    for w, slab in to_cast:
        _, rows, cols = w.shape
        assert rows % steps == 0
        cast_in.append(pl.BlockSpec((None, rows // steps, cols),
                                    lambda i, j, slab=slab: (slab, i * nt + j, 0)))
        cast_out.append(pl.BlockSpec((rows // steps, cols), lambda i, j: (i * nt + j, 0)))
        cast_shapes.append(jax.ShapeDtypeStruct((rows, cols), BF16))
    return pl.pallas_call(
        functools.partial(_layer0_prompt_kernel, layer=layer, n_cast=len(to_cast)),
        grid=(b, nt),
        in_specs=[tile] + specs + [any_spec] * len(big) + cast_in,
        out_specs=[tile, pool, conv] + [any_spec] * len(big) + cast_out,
        out_shape=[jax.ShapeDtypeStruct(x.shape, F32),
                   jax.ShapeDtypeStruct((1, b, POOL_PREV, D_POOL), F32),
                   jax.ShapeDtypeStruct((1, b, CONV_W - 1, D_CONV), F32)]
        + [jax.ShapeDtypeStruct(sh, BF16) for sh in big_shapes] + cast_shapes,
        scratch_shapes=[pltpu.VMEM((HALO, D_POOL), F32), pltpu.VMEM((HALO, D_CONV), F32),
                        pltpu.VMEM((D_POOL + D_CONV, d), BF16)]
        + [pltpu.VMEM(sh, BF16) for sh in big_shapes]
        + [pltpu.VMEM((2, STAGE_ROWS, STAGE_COLS), F32), pltpu.SemaphoreType.DMA((2,)),
           pltpu.SemaphoreType.DMA((len(big),))],
        compiler_params=_params(2),
        name="layer0_prompt",
    )(x, *arrs, *big, *[w for w, _ in to_cast])


def _layer0_sample(x, state_pool, state_conv, weights, layer, seq):
    n, d = x.shape
    tm = TILE_M_SAMPLE
    tb = tm // seq
    arrs, specs = _split(weights)
    tile = pl.BlockSpec((tm, d), lambda i: (i, 0))
    pool_t = jnp.swapaxes(state_pool, 1, 2)
    pool = pl.BlockSpec((None, POOL_PREV, tb, D_POOL), lambda i: (0, 0, i, 0))
    conv = pl.BlockSpec((None, tb, CONV_W - 1, D_CONV), lambda i: (0, i, 0, 0))
    y, pool_new, conv_new = pl.pallas_call(
        functools.partial(_layer0_sample_kernel, layer=layer, seq=seq),
        grid=(n // tm,),
        in_specs=[tile, pool, conv] + specs,
        out_specs=[tile, pool, conv],
        out_shape=[jax.ShapeDtypeStruct(x.shape, F32),
                   jax.ShapeDtypeStruct(pool_t.shape, F32),
                   jax.ShapeDtypeStruct(state_conv.shape, F32)],
        scratch_shapes=[pltpu.VMEM((tb, HALO + seq, D_POOL), F32),
                        pltpu.VMEM((tb, HALO + seq, D_CONV), F32),
                        pltpu.VMEM((D_POOL + D_CONV, d), BF16)],
        compiler_params=_params(1),
        name="layer0_sample",
    )(x, pool_t, state_conv, *arrs)
    return y, jnp.swapaxes(pool_new, 1, 2), conv_new


def _layer1(x, weights, tm, layer, seq, emit_v, name):
    n, d = x.shape
    arrs, specs = _split(weights)
    tile = pl.BlockSpec((tm, d), lambda i: (i, 0))
    out_specs = [tile, tile] if emit_v else [tile]
    out_shape = [jax.ShapeDtypeStruct(x.shape, F32)] * len(out_specs)
    return pl.pallas_call(
        functools.partial(_layer1_kernel, layer=layer, seq=seq, emit_v=emit_v),
        grid=(n // tm,),
        in_specs=[tile] + specs,
        out_specs=out_specs,
        out_shape=out_shape,
        scratch_shapes=[pltpu.VMEM((SGU_HEADS, CHUNK, CHUNK), BF16),
                        pltpu.VMEM((CHUNK, D_SGU), F32)],
        compiler_params=_params(1),
        name=name,
    )(x, *arrs)


def kernel(x_prompt, x_sample, state_pool, state_conv, g_mix_pre, g_mix_post, g_ffn_pre,
           g_ffn_post, w_in_ab, w_pool_grp, pool_scale, conv_w, w_out_ab, w_uv, g_v, w_spatial,
           b_spatial, w_out_c, w_up, w_down):
    bp, sp, d = x_prompt.shape
    bs, ss, _ = x_sample.shape
    assert ss < CHUNK and CHUNK % ss == 0 and CONV_W - 1 <= ss and sp % CHUNK == 0

    gains = ((g_mix_pre, None), (g_mix_post, None), (g_ffn_pre, None), (g_ffn_post, None))
    w0_small = gains + ((w_pool_grp, 0), (pool_scale, None), (conv_w, 0))

    (y0, pool_p, conv_p, w_in_b, w_out_b, w_up0_b, w_down0_b,
     w_uv_b, w_out_c_b, w_up1_b, w_down1_b) = _layer0_prompt(
        x_prompt, w0_small, (w_in_ab, w_out_ab, w_up, w_down), 0,
        ((w_uv, 0), (w_out_c, 0), (w_up, 1), (w_down, 1)))
    w0 = w0_small + ((w_in_b, None), (w_out_b, None), (w_up0_b, None), (w_down0_b, None))
    w1 = gains + ((g_v, None), (w_spatial, 0), (b_spatial, 0),
                  (w_uv_b, None), (w_out_c_b, None), (w_up1_b, None), (w_down1_b, None))
    (y_prompt,) = _layer1(y0.reshape(bp * sp, d), w1, TILE_M_PROMPT_L1, 1, None, False,
                          "layer1_prompt")

    ys0, pool_s, conv_s = _layer0_sample(x_sample.reshape(bs * ss, d), state_pool, state_conv,
                                         w0, 0, ss)
    ys1, v_s = _layer1(ys0, w1, TILE_M_SAMPLE, 1, ss, True, "layer1_sample")
    return (y_prompt.reshape(bp, sp, d), ys1.reshape(bs, ss, d), pool_p, pool_s, conv_p, conv_s,
            v_s.reshape(1, bs, ss, D_SGU))
```

```python
import functools

import jax
import jax.numpy as jnp
from jax import lax
from jax.experimental import pallas as pl
from jax.experimental.pallas import tpu as pltpu

D_MODEL = 1024
EPS = 1e-6
D_POOL = D_MODEL // 2
POOL_WINDOWS = (2, 4, 8, 16)
POOL_GC = D_POOL // len(POOL_WINDOWS)
POOL_PREV = max(POOL_WINDOWS) - 1
D_CONV = D_MODEL // 2
CONV_W = 3
D_SGU = D_MODEL
SGU_HEADS = 8
SGU_HD = D_SGU // SGU_HEADS
CHUNK = 128
PAST_LEN = 16384

HALO = 16
TILE_M_PROMPT = 512
TILE_M_SAMPLE = 512
SUB_TILE_ROWS = 256
SUB_GROUP = 2
SUB_TILES_SAMPLE0 = 1
STAGE_ROWS, STAGE_COLS = 512, 1024
VMEM_LIMIT_BYTES = 60 * 1024 * 1024

F32 = jnp.float32
BF16 = jnp.bfloat16


def _rms(x, g):
    return x * lax.rsqrt(jnp.mean(x * x, axis=-1, keepdims=True) + EPS) * g


def _dot(a, b):
    return jnp.dot(a, b, preferred_element_type=F32)


def _shift_rows(a, k):
    return pltpu.roll(a, k, axis=0)


def _window_sums(full_u):
    sums = []
    for g, w in enumerate(POOL_WINDOWS):
        s = full_u[:, g * POOL_GC:(g + 1) * POOL_GC]
        k = 1
        while k < w:
            s = s + _shift_rows(s, k)
            k *= 2
        sums.append(s)
    return sums


def _pool_diff(sums, u, pos):
    outs = []
    for g, w in enumerate(POOL_WINDOWS):
        inv_cnt = 1.0 / jnp.minimum(pos + 1.0, float(w))
        outs.append(sums[g] * inv_cnt - u[:, g * POOL_GC:(g + 1) * POOL_GC])
    return jnp.concatenate(outs, axis=1)


def _fold_pool_into_out(w_grp_ref, scale_ref, w_out_ref, w_eff_scr):
    scale = scale_ref[...]
    for g in range(len(POOL_WINDOWS)):
        rows = slice(g * POOL_GC, (g + 1) * POOL_GC)
        a = w_grp_ref[g] * scale[:, rows]
        b = w_out_ref[rows, :].astype(F32)
        w_eff_scr[rows, :] = jnp.dot(a, b, precision=lax.Precision.HIGHEST,
                                     preferred_element_type=F32).astype(BF16)
    w_eff_scr[D_POOL:, :] = w_out_ref[D_POOL:, :]


def _conv_rows(full_cx, conv_w):
    return (conv_w[0:1] * _shift_rows(full_cx, 2) + conv_w[1:2] * _shift_rows(full_cx, 1)
            + conv_w[2:3] * full_cx)


def _gains(gain_refs, layer):
    return [r[layer:layer + 1, :] for r in gain_refs]


def _in_proj(x, g_pre, w_in_ref):
    p = _dot(_rms(x, g_pre).astype(BF16), w_in_ref[...])
    u = p[:, :D_POOL]
    cx = p[:, D_POOL + 2 * D_CONV:] * p[:, D_POOL:D_POOL + D_CONV]
    gate_b = p[:, D_POOL + D_CONV:D_POOL + 2 * D_CONV]
    return u, cx, gate_b


def _ffn(x, g_pre, g_post, w_up_ref, w_down_ref):
    h = _rms(x, g_pre).astype(BF16)
    a = jnp.maximum(_dot(h, w_up_ref[...]), 0.0)
    a = (a * a).astype(BF16)
    f = _dot(a, w_down_ref[...])
    return x + _rms(f, g_post)


def _sub_spans(tm):
    assert tm % SUB_TILE_ROWS == 0
    return [(r0, SUB_TILE_ROWS) for r0 in range(0, tm, SUB_TILE_ROWS)]


def _span_groups(tm):
    spans = _sub_spans(tm)
    return [spans[k:k + SUB_GROUP] for k in range(0, len(spans), SUB_GROUP)]


def _ffn_staged(xs, g_pre, g_post, w_up_ref, w_down_ref):
    acts = []
    for x in xs:
        a = jnp.maximum(_dot(_rms(x, g_pre).astype(BF16), w_up_ref[...]), 0.0)
        acts.append((a * a).astype(BF16))
    return [x + _rms(_dot(a, w_down_ref[...]), g_post) for x, a in zip(xs, acts)]


def _stage_chunks(shape):
    rows, cols = shape
    assert rows % STAGE_ROWS == 0 and cols % STAGE_COLS == 0
    return [(r, c) for r in range(0, rows, STAGE_ROWS) for c in range(0, cols, STAGE_COLS)]


def _stream_cast(srcs, slab, dsts, stage, sem):
    plan = [(src, dst, r0, c0) for src, dst in zip(srcs, dsts) for r0, c0 in _stage_chunks(dst.shape)]

    def fetch(k):
        src, _, r0, c0 = plan[k]
        return pltpu.make_async_copy(
            src.at[slab, pl.ds(r0, STAGE_ROWS), pl.ds(c0, STAGE_COLS)], stage.at[k % 2],
            sem.at[k % 2])

    fetch(0).start()
    for k, (_, dst, r0, c0) in enumerate(plan):
        if k + 1 < len(plan):
            fetch(k + 1).start()
        fetch(k).wait()
        dst[pl.ds(r0, STAGE_ROWS), pl.ds(c0, STAGE_COLS)] = stage[k % 2].astype(BF16)


def _export_copies(bufs, outs, sem):
    return [pltpu.make_async_copy(buf, out, sem.at[j]) for j, (buf, out) in enumerate(zip(bufs, outs))]


def _layer0_prompt_kernel(x_ref, g0, g1, g2, g3, w_grp_ref, scale_ref, conv_w_ref,
                          *rest, layer, n_cast):
    w_hbm, cast_in, rest = rest[:4], rest[4:4 + n_cast], rest[4 + n_cast:]
    (y_ref, pool_out_ref, conv_out_ref), w_exp, rest = rest[:3], rest[3:7], rest[7:]
    cast_out, rest = rest[:n_cast], rest[n_cast:]
    halo_u, halo_cx, w_eff_scr, w_in_ref, w_out_ref, w_up_ref, w_down_ref, stage, sem_in, sem_out = rest
    w_bufs = (w_in_ref, w_out_ref, w_up_ref, w_down_ref)
    tm = x_ref.shape[0]
    t = pl.program_id(1)
    first = (pl.program_id(0) == 0) & (t == 0)
    last = (pl.program_id(0) == pl.num_programs(0) - 1) & (t == pl.num_programs(1) - 1)
    g_pre, g_post, gf_pre, gf_post = _gains((g0, g1, g2, g3), layer)

    for src, dst in zip(cast_in, cast_out):
        dst[...] = src[...].astype(BF16)

    @pl.when(first)
    def _():
        _stream_cast(w_hbm, layer // 2, w_bufs, stage, sem_in)
        for c in _export_copies(w_bufs, w_exp, sem_out):
            c.start()
        _fold_pool_into_out(w_grp_ref, scale_ref, w_out_ref, w_eff_scr)

    @pl.when(last)
    def _():
        for c in _export_copies(w_bufs, w_exp, sem_out):
            c.wait()

    @pl.when(t == 0)
    def _():
        halo_u[...] = jnp.zeros_like(halo_u)
        halo_cx[...] = jnp.zeros_like(halo_cx)

    hist_u, hist_cx = halo_u[...], halo_cx[...]
    for spans in _span_groups(tm):
        xs = [x_ref[pl.ds(r0, ts), :] for r0, ts in spans]
        proj = [_in_proj(x, g_pre, w_in_ref) for x in xs]

        yabs = []
        for (r0, ts), (u, cx, gate_b) in zip(spans, proj):
            full_u = jnp.concatenate([hist_u, u], axis=0)
            full_cx = jnp.concatenate([hist_cx, cx], axis=0)
            hist_u, hist_cx = u[ts - HALO:], cx[ts - HALO:]
            pos = (lax.broadcasted_iota(jnp.int32, (ts, 1), 0) + (t * tm + r0)).astype(F32)
            sums = [s[HALO:] for s in _window_sums(full_u)]
            d = _pool_diff(sums, u, pos)
            cz = _conv_rows(full_cx, conv_w_ref[...])[HALO:]
            yabs.append(jnp.concatenate([d, gate_b * cz], axis=1).astype(BF16))

        x1 = [x + _rms(_dot(yab, w_eff_scr[...]), g_post) for x, yab in zip(xs, yabs)]
        for (r0, ts), y in zip(spans, _ffn_staged(x1, gf_pre, gf_post, w_up_ref, w_down_ref)):
            y_ref[pl.ds(r0, ts), :] = y
    halo_u[...] = hist_u
    halo_cx[...] = hist_cx

    pool_out_ref[...] = halo_u[HALO - POOL_PREV:, :]
    conv_out_ref[...] = halo_cx[HALO - (CONV_W - 1):, :]


def _layer0_sample_kernel(x_ref, pool_ref, conv_ref, g0, g1, g2, g3, w_grp_ref, scale_ref,
                          conv_w_ref, w_in_ref, w_out_ref, w_up_ref, w_down_ref,
                          y_ref, pool_out_ref, conv_out_ref,
                          full_u_scr, full_cx_scr, w_eff_scr, *, layer, seq):
    tm = x_ref.shape[0]
    tb = tm // seq
    grp = HALO + seq
    g_pre, g_post, gf_pre, gf_post = _gains((g0, g1, g2, g3), layer)

    @pl.when(pl.program_id(0) == 0)
    def _():
        _fold_pool_into_out(w_grp_ref, scale_ref, w_out_ref, w_eff_scr)

    ts = tm // SUB_TILES_SAMPLE0
    tbs = tb // SUB_TILES_SAMPLE0
    subs = range(SUB_TILES_SAMPLE0)
    xs = [x_ref[pl.ds(sub * ts, ts), :] for sub in subs]
    proj = [_in_proj(x, g_pre, w_in_ref) for x in xs]

    def new_rows(a):
        return a.reshape(tbs, grp, a.shape[-1])[:, HALO:].reshape(ts, a.shape[-1])

    pos = (lax.broadcasted_iota(jnp.int32, (tbs, seq, 1), 1).reshape(ts, 1) + PAST_LEN).astype(F32)
    yabs = []
    for sub, (u, cx, gate_b) in enumerate(proj):
        sq = pl.ds(sub * tbs, tbs)
        full_u_scr[sq, :HALO - POOL_PREV, :] = jnp.zeros((tbs, HALO - POOL_PREV, D_POOL), F32)
        for r in range(POOL_PREV):
            full_u_scr[sq, HALO - POOL_PREV + r, :] = pool_ref[r, sq, :]
        full_u_scr[sq, HALO:, :] = u.reshape(tbs, seq, D_POOL)
        full_cx_scr[sq, :HALO - (CONV_W - 1), :] = jnp.zeros((tbs, HALO - (CONV_W - 1), D_CONV), F32)
        full_cx_scr[sq, HALO - (CONV_W - 1):HALO, :] = conv_ref[sq]
        full_cx_scr[sq, HALO:, :] = cx.reshape(tbs, seq, D_CONV)
        for r in range(POOL_PREV):
            pool_out_ref[r, sq, :] = full_u_scr[sq, grp - POOL_PREV + r, :]
        conv_out_ref[sq] = full_cx_scr[sq, grp - (CONV_W - 1):, :]
        full_u = full_u_scr[sq].reshape(tbs * grp, D_POOL)
        full_cx = full_cx_scr[sq].reshape(tbs * grp, D_CONV)
        sums = [new_rows(s) for s in _window_sums(full_u)]
        d = _pool_diff(sums, u, pos)
        cz = new_rows(_conv_rows(full_cx, conv_w_ref[...]))
        yabs.append(jnp.concatenate([d, gate_b * cz], axis=1).astype(BF16))

    x1 = [x + _rms(_dot(yab, w_eff_scr[...]), g_post) for x, yab in zip(xs, yabs)]
    for sub, y in enumerate(_ffn_staged(x1, gf_pre, gf_post, w_up_ref, w_down_ref)):
        y_ref[pl.ds(sub * ts, ts), :] = y


def _sgu_tables(w_sp_ref, b_sp_ref, wm_scr, bias_scr, seq):
    r = lax.broadcasted_iota(jnp.int32, (CHUNK, CHUNK), 0)
    c = lax.broadcasted_iota(jnp.int32, (CHUNK, CHUNK), 1)
    keep = r >= c
    b = b_sp_ref[...]
    if seq is not None:
        keep = keep & (r // seq == c // seq)
        pick = (r % seq == c).astype(BF16)
        pick_t = (c % seq == r).astype(BF16)
        b = jnp.where(lax.broadcasted_iota(jnp.int32, b.shape, 1) < seq, b, 0.0)
        k = seq
        while k < CHUNK:
            b = b + pltpu.roll(b, k, axis=1)
            k *= 2
    mask = keep.astype(F32)
    for hh in range(SGU_HEADS):
        w = w_sp_ref[hh]
        if seq is not None:
            w = _dot(_dot(pick, w.astype(BF16)).astype(BF16), pick_t)
        wm_scr[hh] = (w * mask).astype(BF16)
        rows = jnp.broadcast_to(b[hh:hh + 1, :], (CHUNK, SGU_HD))
        bias_scr[:, hh * SGU_HD:(hh + 1) * SGU_HD] = rows.T


def _layer1_kernel(xp_ref, xs_ref, g0, g1, g2, g3, g_v_ref, w_sp_ref, b_sp_ref,
                   w_uv_ref, w_out_ref, w_up_ref, w_down_ref, yp_ref, ys_ref, vs_ref,
                   wm_p, bias_p, wm_s, bias_s, *, layer, seq, n_prompt):
    i = pl.program_id(0)
    gains = _gains((g0, g1, g2, g3), layer)
    weights = (g_v_ref, w_uv_ref, w_out_ref, w_up_ref, w_down_ref)

    @pl.when(i == 0)
    def _():
        _sgu_tables(w_sp_ref, b_sp_ref, wm_p, bias_p, None)
        _sgu_tables(w_sp_ref, b_sp_ref, wm_s, bias_s, seq)

    @pl.when(i < n_prompt)
    def _():
        _layer1_tile(xp_ref, yp_ref, None, wm_p, bias_p, gains, *weights)

    @pl.when(i >= n_prompt)
    def _():
        _layer1_tile(xs_ref, ys_ref, vs_ref, wm_s, bias_s, gains, *weights)


def _layer1_tile(x_ref, y_ref, v_ref, wm_scr, bias_scr, gains, g_v_ref, w_uv_ref, w_out_ref,
                 w_up_ref, w_down_ref):
    tm = x_ref.shape[0]
    emit_v = v_ref is not None
    g_pre, g_post, gf_pre, gf_post = gains

    bias = bias_scr[...]
    for spans in _span_groups(tm):
        xs = [x_ref[pl.ds(r0, ts), :] for r0, ts in spans]
        ps = [_dot(_rms(x, g_pre).astype(BF16), w_uv_ref[...]) for x in xs]

        gated = []
        for (r0, ts), p in zip(spans, ps):
            nc = ts // CHUNK
            u = p[:, :D_SGU]
            v = _rms(p[:, D_SGU:], g_v_ref[...])
            if emit_v:
                v_ref[pl.ds(r0, ts), :] = v
            vb = v.astype(BF16)
            cols = []
            for hh in range(SGU_HEADS):
                lanes = slice(hh * SGU_HD, (hh + 1) * SGU_HD)
                rhs = jnp.concatenate(
                    [vb[ci * CHUNK:(ci + 1) * CHUNK, lanes] for ci in range(nc)], axis=1)
                out = _dot(wm_scr[hh], rhs)
                cols.append(jnp.concatenate(
                    [out[:, ci * SGU_HD:(ci + 1) * SGU_HD] + bias[:, lanes] for ci in range(nc)],
                    axis=0))
            gated.append((u * jnp.concatenate(cols, axis=1)).astype(BF16))

        x1 = [x + _rms(_dot(a, w_out_ref[...]), g_post) for x, a in zip(xs, gated)]
        for (r0, ts), y in zip(spans, _ffn_staged(x1, gf_pre, gf_post, w_up_ref, w_down_ref)):
            y_ref[pl.ds(r0, ts), :] = y


def _resident(arr, layer=None):
    if layer is None:
        nd = arr.ndim
        return pl.BlockSpec(arr.shape, lambda *_: (0,) * nd, pipeline_mode=pl.Buffered(1))
    nd = arr.ndim - 1
    return pl.BlockSpec((None,) + arr.shape[1:], lambda *_: (layer,) + (0,) * nd,
                        pipeline_mode=pl.Buffered(1))


def _params(n_axes):
    return pltpu.CompilerParams(dimension_semantics=("arbitrary",) * n_axes,
                                vmem_limit_bytes=VMEM_LIMIT_BYTES)


def _split(weights):
    return [a for a, _ in weights], [_resident(a, l) for a, l in weights]


def _layer0_prompt(x, weights, big, layer, to_cast):
    b, s, d = x.shape
    tm = TILE_M_PROMPT
    nt = s // tm
    steps = b * nt
    arrs, specs = _split(weights)
    any_spec = pl.BlockSpec(memory_space=pl.ANY)
    big_shapes = [w.shape[1:] for w in big]
    tile = pl.BlockSpec((None, tm, d), lambda i, j: (i, j, 0))
    pool = pl.BlockSpec((None, None, POOL_PREV, D_POOL), lambda i, j: (0, i, 0, 0))
    conv = pl.BlockSpec((None, None, CONV_W - 1, D_CONV), lambda i, j: (0, i, 0, 0))
    cast_in, cast_out, cast_shapes = [], [], []
    for w, slab in to_cast:
        _, rows, cols = w.shape
        assert rows % steps == 0
        cast_in.append(pl.BlockSpec((None, rows // steps, cols),
                                    lambda i, j, slab=slab: (slab, i * nt + j, 0)))
        cast_out.append(pl.BlockSpec((rows // steps, cols), lambda i, j: (i * nt + j, 0)))
        cast_shapes.append(jax.ShapeDtypeStruct((rows, cols), BF16))
    return pl.pallas_call(
        functools.partial(_layer0_prompt_kernel, layer=layer, n_cast=len(to_cast)),
        grid=(b, nt),
        in_specs=[tile] + specs + [any_spec] * len(big) + cast_in,
        out_specs=[tile, pool, conv] + [any_spec] * len(big) + cast_out,
        out_shape=[jax.ShapeDtypeStruct(x.shape, F32),
                   jax.ShapeDtypeStruct((1, b, POOL_PREV, D_POOL), F32),
                   jax.ShapeDtypeStruct((1, b, CONV_W - 1, D_CONV), F32)]
        + [jax.ShapeDtypeStruct(sh, BF16) for sh in big_shapes] + cast_shapes,
        scratch_shapes=[pltpu.VMEM((HALO, D_POOL), F32), pltpu.VMEM((HALO, D_CONV), F32),
                        pltpu.VMEM((D_POOL + D_CONV, d), BF16)]
        + [pltpu.VMEM(sh, BF16) for sh in big_shapes]
        + [pltpu.VMEM((2, STAGE_ROWS, STAGE_COLS), F32), pltpu.SemaphoreType.DMA((2,)),
           pltpu.SemaphoreType.DMA((len(big),))],
        compiler_params=_params(2),
        name="layer0_prompt",
    )(x, *arrs, *big, *[w for w, _ in to_cast])


def _layer0_sample(x, state_pool, state_conv, weights, layer, seq):
    n, d = x.shape
    tm = TILE_M_SAMPLE
    tb = tm // seq
    arrs, specs = _split(weights)
    tile = pl.BlockSpec((tm, d), lambda i: (i, 0))
    pool_t = jnp.swapaxes(state_pool, 1, 2)
    pool = pl.BlockSpec((None, POOL_PREV, tb, D_POOL), lambda i: (0, 0, i, 0))
    conv = pl.BlockSpec((None, tb, CONV_W - 1, D_CONV), lambda i: (0, i, 0, 0))
    y, pool_new, conv_new = pl.pallas_call(
        functools.partial(_layer0_sample_kernel, layer=layer, seq=seq),
        grid=(n // tm,),
        in_specs=[tile, pool, conv] + specs,
        out_specs=[tile, pool, conv],
        out_shape=[jax.ShapeDtypeStruct(x.shape, F32),
                   jax.ShapeDtypeStruct(pool_t.shape, F32),
                   jax.ShapeDtypeStruct(state_conv.shape, F32)],
        scratch_shapes=[pltpu.VMEM((tb, HALO + seq, D_POOL), F32),
                        pltpu.VMEM((tb, HALO + seq, D_CONV), F32),
                        pltpu.VMEM((D_POOL + D_CONV, d), BF16)],
        compiler_params=_params(1),
        name="layer0_sample",
    )(x, pool_t, state_conv, *arrs)
    return y, jnp.swapaxes(pool_new, 1, 2), conv_new


def _layer1(x_p, x_s, weights, layer, seq):
    d = x_p.shape[1]
    tm = TILE_M_PROMPT
    n_p, n_s = x_p.shape[0] // tm, x_s.shape[0] // tm
    arrs, specs = _split(weights)
    tile_p = pl.BlockSpec((tm, d), lambda i: (jnp.minimum(i, n_p - 1), 0))
    tile_s = pl.BlockSpec((tm, d), lambda i: (jnp.maximum(i - n_p, 0), 0))
    tables = [pltpu.VMEM((SGU_HEADS, CHUNK, CHUNK), BF16), pltpu.VMEM((CHUNK, D_SGU), F32)]
    return pl.pallas_call(
        functools.partial(_layer1_kernel, layer=layer, seq=seq, n_prompt=n_p),
        grid=(n_p + n_s,),
        in_specs=[tile_p, tile_s] + specs,
        out_specs=[tile_p, tile_s, tile_s],
        out_shape=[jax.ShapeDtypeStruct(x_p.shape, F32), jax.ShapeDtypeStruct(x_s.shape, F32),
                   jax.ShapeDtypeStruct(x_s.shape, F32)],
        scratch_shapes=tables + tables,
        compiler_params=_params(1),
        name="layer1",
    )(x_p, x_s, *arrs)


def kernel(x_prompt, x_sample, state_pool, state_conv, g_mix_pre, g_mix_post, g_ffn_pre,
           g_ffn_post, w_in_ab, w_pool_grp, pool_scale, conv_w, w_out_ab, w_uv, g_v, w_spatial,
           b_spatial, w_out_c, w_up, w_down):
    bp, sp, d = x_prompt.shape
    bs, ss, _ = x_sample.shape
    assert ss < CHUNK and CHUNK % ss == 0 and CONV_W - 1 <= ss and sp % CHUNK == 0

    gains = ((g_mix_pre, None), (g_mix_post, None), (g_ffn_pre, None), (g_ffn_post, None))
    w0_small = gains + ((w_pool_grp, 0), (pool_scale, None), (conv_w, 0))

    (y0, pool_p, conv_p, w_in_b, w_out_b, w_up0_b, w_down0_b,
     w_uv_b, w_out_c_b, w_up1_b, w_down1_b) = _layer0_prompt(
        x_prompt, w0_small, (w_in_ab, w_out_ab, w_up, w_down), 0,
        ((w_uv, 0), (w_out_c, 0), (w_up, 1), (w_down, 1)))
    w0 = w0_small + ((w_in_b, None), (w_out_b, None), (w_up0_b, None), (w_down0_b, None))
    w1 = gains + ((g_v, None), (w_spatial, 0), (b_spatial, 0),
                  (w_uv_b, None), (w_out_c_b, None), (w_up1_b, None), (w_down1_b, None))
    ys0, pool_s, conv_s = _layer0_sample(x_sample.reshape(bs * ss, d), state_pool, state_conv,
                                         w0, 0, ss)
    y_prompt, ys1, v_s = _layer1(y0.reshape(bp * sp, d), ys0, w1, 1, ss)
    return (y_prompt.reshape(bp, sp, d), ys1.reshape(bs, ss, d), pool_p, pool_s, conv_p, conv_s,
            v_s.reshape(1, bs, ss, D_SGU))
```

```python
import functools

import jax
import jax.numpy as jnp
from jax import lax
from jax.experimental import pallas as pl
from jax.experimental.pallas import tpu as pltpu

D_MODEL = 1024
EPS = 1e-6
D_POOL = D_MODEL // 2
POOL_WINDOWS = (2, 4, 8, 16)
POOL_GC = D_POOL // len(POOL_WINDOWS)
POOL_PREV = max(POOL_WINDOWS) - 1
D_CONV = D_MODEL // 2
CONV_W = 3
D_SGU = D_MODEL
SGU_HEADS = 8
SGU_HD = D_SGU // SGU_HEADS
CHUNK = 128
PAST_LEN = 16384

HALO = 16
TILE_M_PROMPT = 512
TILE_M_SAMPLE = 512
SUB_TILE_ROWS = 256
SUB_GROUP = 2
SUB_TILES_SAMPLE0 = 1
STAGE_ROWS, STAGE_COLS = 512, 1024
VMEM_LIMIT_BYTES = 60 * 1024 * 1024

F32 = jnp.float32
BF16 = jnp.bfloat16


def _rms(x, g):
    return x * lax.rsqrt(jnp.mean(x * x, axis=-1, keepdims=True) + EPS) * g


def _dot(a, b):
    return jnp.dot(a, b, preferred_element_type=F32)


def _shift_rows(a, k):
    return pltpu.roll(a, k, axis=0)


def _window_sums(full_u):
    sums = []
    for g, w in enumerate(POOL_WINDOWS):
        s = full_u[:, g * POOL_GC:(g + 1) * POOL_GC]
        k = 1
        while k < w:
            s = s + _shift_rows(s, k)
            k *= 2
        sums.append(s)
    return sums


def _pool_diff(sums, u, pos):
    outs = []
    for g, w in enumerate(POOL_WINDOWS):
        inv_cnt = 1.0 / jnp.minimum(pos + 1.0, float(w))
        outs.append(sums[g] * inv_cnt - u[:, g * POOL_GC:(g + 1) * POOL_GC])
    return jnp.concatenate(outs, axis=1)


def _fold_pool_into_out(w_grp_ref, scale_ref, w_out_ref, w_eff_scr):
    scale = scale_ref[...]
    for g in range(len(POOL_WINDOWS)):
        rows = slice(g * POOL_GC, (g + 1) * POOL_GC)
        a = w_grp_ref[g] * scale[:, rows]
        a_hi = a.astype(BF16)
        a_lo = (a - a_hi.astype(F32)).astype(BF16)
        b = w_out_ref[rows, :]
        w_eff_scr[rows, :] = (_dot(a_hi, b) + _dot(a_lo, b)).astype(BF16)
    w_eff_scr[D_POOL:, :] = w_out_ref[D_POOL:, :]


def _conv_rows(full_cx, conv_w):
    return (conv_w[0:1] * _shift_rows(full_cx, 2) + conv_w[1:2] * _shift_rows(full_cx, 1)
            + conv_w[2:3] * full_cx)


def _gains(gain_refs, layer):
    return [r[layer:layer + 1, :] for r in gain_refs]


def _in_proj(x, g_pre, w_in_ref):
    p = _dot(_rms(x, g_pre).astype(BF16), w_in_ref[...])
    u = p[:, :D_POOL]
    cx = p[:, D_POOL + 2 * D_CONV:] * p[:, D_POOL:D_POOL + D_CONV]
    gate_b = p[:, D_POOL + D_CONV:D_POOL + 2 * D_CONV]
    return u, cx, gate_b


def _ffn(x, g_pre, g_post, w_up_ref, w_down_ref):
    h = _rms(x, g_pre).astype(BF16)
    a = jnp.maximum(_dot(h, w_up_ref[...]), 0.0)
    a = (a * a).astype(BF16)
    f = _dot(a, w_down_ref[...])
    return x + _rms(f, g_post)


def _sub_spans(tm):
    assert tm % SUB_TILE_ROWS == 0
    return [(r0, SUB_TILE_ROWS) for r0 in range(0, tm, SUB_TILE_ROWS)]


def _span_groups(tm):
    spans = _sub_spans(tm)
    return [spans[k:k + SUB_GROUP] for k in range(0, len(spans), SUB_GROUP)]


def _ffn_staged(xs, g_pre, g_post, w_up_ref, w_down_ref):
    acts = []
    for x in xs:
        a = jnp.maximum(_dot(_rms(x, g_pre).astype(BF16), w_up_ref[...]), 0.0)
        acts.append((a * a).astype(BF16))
    return [x + _rms(_dot(a, w_down_ref[...]), g_post) for x, a in zip(xs, acts)]


def _stream_cast(srcs, slab, dsts, stage, sem):
    def make_fetch(src, dst):
        rows, cols = dst.shape
        assert rows % STAGE_ROWS == 0 and cols % STAGE_COLS == 0
        ncol = cols // STAGE_COLS
        n = (rows // STAGE_ROWS) * ncol
        assert n % 2 == 0

        def window(k):
            r0 = pl.multiple_of((k // ncol) * STAGE_ROWS, STAGE_ROWS)
            c0 = pl.multiple_of((k % ncol) * STAGE_COLS, STAGE_COLS)
            return pl.ds(r0, STAGE_ROWS), pl.ds(c0, STAGE_COLS)

        def fetch(k):
            rs, cs = window(k)
            return pltpu.make_async_copy(src.at[slab, rs, cs], stage.at[k % 2], sem.at[k % 2])

        return n, window, fetch

    plans = [make_fetch(src, dst) for src, dst in zip(srcs, dsts)]
    plans[0][2](0).start()
    for j, ((n, window, fetch), dst) in enumerate(zip(plans, dsts)):
        next_fetch = plans[j + 1][2] if j + 1 < len(plans) else None

        def body(k, carry, n=n, window=window, fetch=fetch, dst=dst, next_fetch=next_fetch):
            @pl.when(k + 1 < n)
            def _():
                fetch(k + 1).start()

            if next_fetch is not None:
                @pl.when(k + 1 == n)
                def _():
                    next_fetch(0).start()

            fetch(k).wait()
            rs, cs = window(k)
            dst[rs, cs] = stage[k % 2].astype(BF16)
            return carry

        lax.fori_loop(0, n, body, 0)


def _export_copies(bufs, outs, sem):
    return [pltpu.make_async_copy(buf, out, sem.at[j]) for j, (buf, out) in enumerate(zip(bufs, outs))]


def _layer0_prompt_kernel(x_ref, g0, g1, g2, g3, w_grp_ref, scale_ref, conv_w_ref,
                          *rest, layer, n_cast):
    w_hbm, cast_in, rest = rest[:4], rest[4:4 + n_cast], rest[4 + n_cast:]
    (y_ref, pool_out_ref, conv_out_ref), w_exp, rest = rest[:3], rest[3:7], rest[7:]
    cast_out, rest = rest[:n_cast], rest[n_cast:]
    halo_u, halo_cx, w_eff_scr, w_in_ref, w_out_ref, w_up_ref, w_down_ref, stage, sem_in, sem_out = rest
    w_bufs = (w_in_ref, w_out_ref, w_up_ref, w_down_ref)
    tm = x_ref.shape[0]
    t = pl.program_id(1)
    first = (pl.program_id(0) == 0) & (t == 0)
    last = (pl.program_id(0) == pl.num_programs(0) - 1) & (t == pl.num_programs(1) - 1)
    g_pre, g_post, gf_pre, gf_post = _gains((g0, g1, g2, g3), layer)

    for src, dst in zip(cast_in, cast_out):
        dst[...] = src[...].astype(BF16)

    @pl.when(first)
    def _():
        _stream_cast(w_hbm, layer // 2, w_bufs, stage, sem_in)
        for c in _export_copies(w_bufs, w_exp, sem_out):
            c.start()
        _fold_pool_into_out(w_grp_ref, scale_ref, w_out_ref, w_eff_scr)

    @pl.when(last)
    def _():
        for c in _export_copies(w_bufs, w_exp, sem_out):
            c.wait()

    @pl.when(t == 0)
    def _():
        halo_u[...] = jnp.zeros_like(halo_u)
        halo_cx[...] = jnp.zeros_like(halo_cx)

    hist_u, hist_cx = halo_u[...], halo_cx[...]
    for spans in _span_groups(tm):
        xs = [x_ref[pl.ds(r0, ts), :] for r0, ts in spans]
        proj = [_in_proj(x, g_pre, w_in_ref) for x in xs]

        yabs = []
        for (r0, ts), (u, cx, gate_b) in zip(spans, proj):
            full_u = jnp.concatenate([hist_u, u], axis=0)
            full_cx = jnp.concatenate([hist_cx, cx], axis=0)
            hist_u, hist_cx = u[ts - HALO:], cx[ts - HALO:]
            pos = (lax.broadcasted_iota(jnp.int32, (ts, 1), 0) + (t * tm + r0)).astype(F32)
            sums = [s[HALO:] for s in _window_sums(full_u)]
            d = _pool_diff(sums, u, pos)
            cz = _conv_rows(full_cx, conv_w_ref[...])[HALO:]
            yabs.append(jnp.concatenate([d, gate_b * cz], axis=1).astype(BF16))

        x1 = [x + _rms(_dot(yab, w_eff_scr[...]), g_post) for x, yab in zip(xs, yabs)]
        for (r0, ts), y in zip(spans, _ffn_staged(x1, gf_pre, gf_post, w_up_ref, w_down_ref)):
            y_ref[pl.ds(r0, ts), :] = y
    halo_u[...] = hist_u
    halo_cx[...] = hist_cx

    pool_out_ref[...] = halo_u[HALO - POOL_PREV:, :]
    conv_out_ref[...] = halo_cx[HALO - (CONV_W - 1):, :]


def _layer0_sample_kernel(x_ref, pool_ref, conv_ref, g0, g1, g2, g3, w_grp_ref, scale_ref,
                          conv_w_ref, w_in_ref, w_out_ref, w_up_ref, w_down_ref,
                          y_ref, pool_out_ref, conv_out_ref,
                          full_u_scr, full_cx_scr, w_eff_scr, *, layer, seq):
    tm = x_ref.shape[0]
    tb = tm // seq
    grp = HALO + seq
    g_pre, g_post, gf_pre, gf_post = _gains((g0, g1, g2, g3), layer)

    @pl.when(pl.program_id(0) == 0)
    def _():
        _fold_pool_into_out(w_grp_ref, scale_ref, w_out_ref, w_eff_scr)

    ts = tm // SUB_TILES_SAMPLE0
    tbs = tb // SUB_TILES_SAMPLE0
    subs = range(SUB_TILES_SAMPLE0)
    xs = [x_ref[pl.ds(sub * ts, ts), :] for sub in subs]
    proj = [_in_proj(x, g_pre, w_in_ref) for x in xs]

    def new_rows(a):
        return a.reshape(tbs, grp, a.shape[-1])[:, HALO:].reshape(ts, a.shape[-1])

    pos = (lax.broadcasted_iota(jnp.int32, (tbs, seq, 1), 1).reshape(ts, 1) + PAST_LEN).astype(F32)
    yabs = []
    for sub, (u, cx, gate_b) in enumerate(proj):
        sq = pl.ds(sub * tbs, tbs)
        full_u_scr[sq, :HALO - POOL_PREV, :] = jnp.zeros((tbs, HALO - POOL_PREV, D_POOL), F32)
        for r in range(POOL_PREV):
            full_u_scr[sq, HALO - POOL_PREV + r, :] = pool_ref[r, sq, :]
        full_u_scr[sq, HALO:, :] = u.reshape(tbs, seq, D_POOL)
        full_cx_scr[sq, :HALO - (CONV_W - 1), :] = jnp.zeros((tbs, HALO - (CONV_W - 1), D_CONV), F32)
        full_cx_scr[sq, HALO - (CONV_W - 1):HALO, :] = conv_ref[sq]
        full_cx_scr[sq, HALO:, :] = cx.reshape(tbs, seq, D_CONV)
        for r in range(POOL_PREV):
            pool_out_ref[r, sq, :] = full_u_scr[sq, grp - POOL_PREV + r, :]
        conv_out_ref[sq] = full_cx_scr[sq, grp - (CONV_W - 1):, :]
        full_u = full_u_scr[sq].reshape(tbs * grp, D_POOL)
        full_cx = full_cx_scr[sq].reshape(tbs * grp, D_CONV)
        sums = [new_rows(s) for s in _window_sums(full_u)]
        d = _pool_diff(sums, u, pos)
        cz = new_rows(_conv_rows(full_cx, conv_w_ref[...]))
        yabs.append(jnp.concatenate([d, gate_b * cz], axis=1).astype(BF16))

    x1 = [x + _rms(_dot(yab, w_eff_scr[...]), g_post) for x, yab in zip(xs, yabs)]
    for sub, y in enumerate(_ffn_staged(x1, gf_pre, gf_post, w_up_ref, w_down_ref)):
        y_ref[pl.ds(sub * ts, ts), :] = y


def _sgu_tables(w_sp_ref, b_sp_ref, wm_scr, bias_scr, seq):
    r = lax.broadcasted_iota(jnp.int32, (CHUNK, CHUNK), 0)
    c = lax.broadcasted_iota(jnp.int32, (CHUNK, CHUNK), 1)
    keep = r >= c
    b = b_sp_ref[...]
    if seq is not None:
        keep = keep & (r // seq == c // seq)
        pick = (r % seq == c).astype(BF16)
        pick_t = (c % seq == r).astype(BF16)
        b = jnp.where(lax.broadcasted_iota(jnp.int32, b.shape, 1) < seq, b, 0.0)
        k = seq
        while k < CHUNK:
            b = b + pltpu.roll(b, k, axis=1)
            k *= 2
    mask = keep.astype(F32)
    for hh in range(SGU_HEADS):
        w = w_sp_ref[hh]
        if seq is not None:
            w = _dot(_dot(pick, w.astype(BF16)).astype(BF16), pick_t)
        wm_scr[hh] = (w * mask).astype(BF16)
        rows = jnp.broadcast_to(b[hh:hh + 1, :], (CHUNK, SGU_HD))
        bias_scr[:, hh * SGU_HD:(hh + 1) * SGU_HD] = rows.T


def _layer1_kernel(x_ref, g0, g1, g2, g3, g_v_ref, w_sp_ref, b_sp_ref,
                   w_uv_ref, w_out_ref, w_up_ref, w_down_ref, y_ref, *rest, layer, seq, emit_v):
    if emit_v:
        v_ref, wm_scr, bias_scr = rest
    else:
        wm_scr, bias_scr = rest
    tm = x_ref.shape[0]
    g_pre, g_post, gf_pre, gf_post = _gains((g0, g1, g2, g3), layer)

    @pl.when(pl.program_id(0) == 0)
    def _():
        _sgu_tables(w_sp_ref, b_sp_ref, wm_scr, bias_scr, seq)

    bias = bias_scr[...]
    for spans in _span_groups(tm):
        xs = [x_ref[pl.ds(r0, ts), :] for r0, ts in spans]
        ps = [_dot(_rms(x, g_pre).astype(BF16), w_uv_ref[...]) for x in xs]

        gated = []
        for (r0, ts), p in zip(spans, ps):
            nc = ts // CHUNK
            u = p[:, :D_SGU]
            v = _rms(p[:, D_SGU:], g_v_ref[...])
            if emit_v:
                v_ref[pl.ds(r0, ts), :] = v
            vb = v.astype(BF16)
            cols = []
            for hh in range(SGU_HEADS):
                lanes = slice(hh * SGU_HD, (hh + 1) * SGU_HD)
                rhs = jnp.concatenate(
                    [vb[ci * CHUNK:(ci + 1) * CHUNK, lanes] for ci in range(nc)], axis=1)
                out = _dot(wm_scr[hh], rhs)
                cols.append(jnp.concatenate(
                    [out[:, ci * SGU_HD:(ci + 1) * SGU_HD] + bias[:, lanes] for ci in range(nc)],
                    axis=0))
            gated.append((u * jnp.concatenate(cols, axis=1)).astype(BF16))

        x1 = [x + _rms(_dot(a, w_out_ref[...]), g_post) for x, a in zip(xs, gated)]
        for (r0, ts), y in zip(spans, _ffn_staged(x1, gf_pre, gf_post, w_up_ref, w_down_ref)):
            y_ref[pl.ds(r0, ts), :] = y


def _resident(arr, layer=None):
    if layer is None:
        nd = arr.ndim
        return pl.BlockSpec(arr.shape, lambda *_: (0,) * nd, pipeline_mode=pl.Buffered(1))
    nd = arr.ndim - 1
    return pl.BlockSpec((None,) + arr.shape[1:], lambda *_: (layer,) + (0,) * nd,
                        pipeline_mode=pl.Buffered(1))


def _params(n_axes):
    return pltpu.CompilerParams(dimension_semantics=("arbitrary",) * n_axes,
                                vmem_limit_bytes=VMEM_LIMIT_BYTES)


def _split(weights):
    return [a for a, _ in weights], [_resident(a, l) for a, l in weights]


def _layer0_prompt(x, weights, big, layer, to_cast):
    b, s, d = x.shape
    tm = TILE_M_PROMPT
    nt = s // tm
    steps = b * nt
    arrs, specs = _split(weights)
    any_spec = pl.BlockSpec(memory_space=pl.ANY)
    big_shapes = [w.shape[1:] for w in big]
    tile = pl.BlockSpec((None, tm, d), lambda i, j: (i, j, 0))
    pool = pl.BlockSpec((None, None, POOL_PREV, D_POOL), lambda i, j: (0, i, 0, 0))
    conv = pl.BlockSpec((None, None, CONV_W - 1, D_CONV), lambda i, j: (0, i, 0, 0))
    cast_in, cast_out, cast_shapes = [], [], []
    for w, slab in to_cast:
        _, rows, cols = w.shape
        assert rows % steps == 0
        cast_in.append(pl.BlockSpec((None, rows // steps, cols),
                                    lambda i, j, slab=slab: (slab, i * nt + j, 0)))
        cast_out.append(pl.BlockSpec((rows // steps, cols), lambda i, j: (i * nt + j, 0)))
        cast_shapes.append(jax.ShapeDtypeStruct((rows, cols), BF16))
    return pl.pallas_call(
        functools.partial(_layer0_prompt_kernel, layer=layer, n_cast=len(to_cast)),
        grid=(b, nt),
        in_specs=[tile] + specs + [any_spec] * len(big) + cast_in,
        out_specs=[tile, pool, conv] + [any_spec] * len(big) + cast_out,
        out_shape=[jax.ShapeDtypeStruct(x.shape, F32),
                   jax.ShapeDtypeStruct((1, b, POOL_PREV, D_POOL), F32),
                   jax.ShapeDtypeStruct((1, b, CONV_W - 1, D_CONV), F32)]
        + [jax.ShapeDtypeStruct(sh, BF16) for sh in big_shapes] + cast_shapes,
        scratch_shapes=[pltpu.VMEM((HALO, D_POOL), F32), pltpu.VMEM((HALO, D_CONV), F32),
                        pltpu.VMEM((D_POOL + D_CONV, d), BF16)]
        + [pltpu.VMEM(sh, BF16) for sh in big_shapes]
        + [pltpu.VMEM((2, STAGE_ROWS, STAGE_COLS), F32), pltpu.SemaphoreType.DMA((2,)),
           pltpu.SemaphoreType.DMA((len(big),))],
        compiler_params=_params(2),
        name="layer0_prompt",
    )(x, *arrs, *big, *[w for w, _ in to_cast])


def _layer0_sample(x, state_pool, state_conv, weights, layer, seq):
    n, d = x.shape
    tm = TILE_M_SAMPLE
    tb = tm // seq
    arrs, specs = _split(weights)
    tile = pl.BlockSpec((tm, d), lambda i: (i, 0))
    pool_t = jnp.swapaxes(state_pool, 1, 2)
    pool = pl.BlockSpec((None, POOL_PREV, tb, D_POOL), lambda i: (0, 0, i, 0))
    conv = pl.BlockSpec((None, tb, CONV_W - 1, D_CONV), lambda i: (0, i, 0, 0))
    y, pool_new, conv_new = pl.pallas_call(
        functools.partial(_layer0_sample_kernel, layer=layer, seq=seq),
        grid=(n // tm,),
        in_specs=[tile, pool, conv] + specs,
        out_specs=[tile, pool, conv],
        out_shape=[jax.ShapeDtypeStruct(x.shape, F32),
                   jax.ShapeDtypeStruct(pool_t.shape, F32),
                   jax.ShapeDtypeStruct(state_conv.shape, F32)],
        scratch_shapes=[pltpu.VMEM((tb, HALO + seq, D_POOL), F32),
                        pltpu.VMEM((tb, HALO + seq, D_CONV), F32),
                        pltpu.VMEM((D_POOL + D_CONV, d), BF16)],
        compiler_params=_params(1),
        name="layer0_sample",
    )(x, pool_t, state_conv, *arrs)
    return y, jnp.swapaxes(pool_new, 1, 2), conv_new


def _layer1(x, weights, tm, layer, seq, emit_v, name):
    n, d = x.shape
    arrs, specs = _split(weights)
    tile = pl.BlockSpec((tm, d), lambda i: (i, 0))
    out_specs = [tile, tile] if emit_v else [tile]
    out_shape = [jax.ShapeDtypeStruct(x.shape, F32)] * len(out_specs)
    return pl.pallas_call(
        functools.partial(_layer1_kernel, layer=layer, seq=seq, emit_v=emit_v),
        grid=(n // tm,),
        in_specs=[tile] + specs,
        out_specs=out_specs,
        out_shape=out_shape,
        scratch_shapes=[pltpu.VMEM((SGU_HEADS, CHUNK, CHUNK), BF16),
                        pltpu.VMEM((CHUNK, D_SGU), F32)],
        compiler_params=_params(1),
        name=name,
    )(x, *arrs)


def kernel(x_prompt, x_sample, state_pool, state_conv, g_mix_pre, g_mix_post, g_ffn_pre,
           g_ffn_post, w_in_ab, w_pool_grp, pool_scale, conv_w, w_out_ab, w_uv, g_v, w_spatial,
           b_spatial, w_out_c, w_up, w_down):
    bp, sp, d = x_prompt.shape
    bs, ss, _ = x_sample.shape
    assert ss < CHUNK and CHUNK % ss == 0 and CONV_W - 1 <= ss and sp % CHUNK == 0

    gains = ((g_mix_pre, None), (g_mix_post, None), (g_ffn_pre, None), (g_ffn_post, None))
    w0_small = gains + ((w_pool_grp, 0), (pool_scale, None), (conv_w, 0))

    (y0, pool_p, conv_p, w_in_b, w_out_b, w_up0_b, w_down0_b,
     w_uv_b, w_out_c_b, w_up1_b, w_down1_b) = _layer0_prompt(
        x_prompt, w0_small, (w_in_ab, w_out_ab, w_up, w_down), 0,
        ((w_uv, 0), (w_out_c, 0), (w_up, 1), (w_down, 1)))
    w0 = w0_small + ((w_in_b, None), (w_out_b, None), (w_up0_b, None), (w_down0_b, None))
    w1 = gains + ((g_v, None), (w_spatial, 0), (b_spatial, 0),
                  (w_uv_b, None), (w_out_c_b, None), (w_up1_b, None), (w_down1_b, None))
    (y_prompt,) = _layer1(y0.reshape(bp * sp, d), w1, TILE_M_PROMPT, 1, None, False,
                          "layer1_prompt")

    ys0, pool_s, conv_s = _layer0_sample(x_sample.reshape(bs * ss, d), state_pool, state_conv,
                                         w0, 0, ss)
    ys1, v_s = _layer1(ys0, w1, TILE_M_SAMPLE, 1, ss, True, "layer1_sample")
    return (y_prompt.reshape(bp, sp, d), ys1.reshape(bs, ss, d), pool_p, pool_s, conv_p, conv_s,
            v_s.reshape(1, bs, ss, D_SGU))
```

```python
import functools

import jax
import jax.numpy as jnp
from jax import lax
from jax.experimental import pallas as pl
from jax.experimental.pallas import tpu as pltpu

D_MODEL = 1024
EPS = 1e-6
D_POOL = D_MODEL // 2
POOL_WINDOWS = (2, 4, 8, 16)
POOL_GC = D_POOL // len(POOL_WINDOWS)
POOL_PREV = max(POOL_WINDOWS) - 1
D_CONV = D_MODEL // 2
CONV_W = 3
D_SGU = D_MODEL
SGU_HEADS = 8
SGU_HD = D_SGU // SGU_HEADS
CHUNK = 128
PAST_LEN = 16384

HALO = 16
TILE_M_PROMPT = 512
TILE_M_SAMPLE = 512
SUB_TILE_ROWS = 256
SUB_GROUP = 2
SUB_TILES_SAMPLE0 = 1
STAGE_ROWS, STAGE_COLS = 512, 1024
VMEM_LIMIT_BYTES = 60 * 1024 * 1024

F32 = jnp.float32
BF16 = jnp.bfloat16


def _rms(x, g):
    return x * lax.rsqrt(jnp.mean(x * x, axis=-1, keepdims=True) + EPS) * g


def _dot(a, b):
    return jnp.dot(a, b, preferred_element_type=F32)


def _shift_rows(a, k):
    return pltpu.roll(a, k, axis=0)


def _window_sums(full_u):
    sums = []
    for g, w in enumerate(POOL_WINDOWS):
        s = full_u[:, g * POOL_GC:(g + 1) * POOL_GC]
        k = 1
        while k < w:
            s = s + _shift_rows(s, k)
            k *= 2
        sums.append(s)
    return sums


def _pool_diff(sums, u, pos):
    outs = []
    for g, w in enumerate(POOL_WINDOWS):
        inv_cnt = 1.0 / jnp.minimum(pos + 1.0, float(w))
        outs.append(sums[g] * inv_cnt - u[:, g * POOL_GC:(g + 1) * POOL_GC])
    return jnp.concatenate(outs, axis=1)


def _fold_pool_into_out(w_grp_ref, scale_ref, w_out_ref, w_eff_scr):
    scale = scale_ref[...]
    for g in range(len(POOL_WINDOWS)):
        rows = slice(g * POOL_GC, (g + 1) * POOL_GC)
        a = w_grp_ref[g] * scale[:, rows]
        a_hi = a.astype(BF16)
        a_lo = (a - a_hi.astype(F32)).astype(BF16)
        b = w_out_ref[rows, :]
        w_eff_scr[rows, :] = (_dot(a_hi, b) + _dot(a_lo, b)).astype(BF16)
    w_eff_scr[D_POOL:, :] = w_out_ref[D_POOL:, :]


def _conv_rows(full_cx, conv_w):
    return (conv_w[0:1] * _shift_rows(full_cx, 2) + conv_w[1:2] * _shift_rows(full_cx, 1)
            + conv_w[2:3] * full_cx)


def _gains(gain_refs, layer):
    return [r[layer:layer + 1, :] for r in gain_refs]


def _in_proj(x, g_pre, w_in_ref):
    p = _dot(_rms(x, g_pre).astype(BF16), w_in_ref[...])
    u = p[:, :D_POOL]
    cx = p[:, D_POOL + 2 * D_CONV:] * p[:, D_POOL:D_POOL + D_CONV]
    gate_b = p[:, D_POOL + D_CONV:D_POOL + 2 * D_CONV]
    return u, cx, gate_b


def _ffn(x, g_pre, g_post, w_up_ref, w_down_ref):
    h = _rms(x, g_pre).astype(BF16)
    a = jnp.maximum(_dot(h, w_up_ref[...]), 0.0)
    a = (a * a).astype(BF16)
    f = _dot(a, w_down_ref[...])
    return x + _rms(f, g_post)


def _sub_spans(tm):
    assert tm % SUB_TILE_ROWS == 0
    return [(r0, SUB_TILE_ROWS) for r0 in range(0, tm, SUB_TILE_ROWS)]


def _span_groups(tm):
    spans = _sub_spans(tm)
    return [spans[k:k + SUB_GROUP] for k in range(0, len(spans), SUB_GROUP)]


def _ffn_staged(xs, g_pre, g_post, w_up_ref, w_down_ref):
    acts = []
    for x in xs:
        a = jnp.maximum(_dot(_rms(x, g_pre).astype(BF16), w_up_ref[...]), 0.0)
        acts.append((a * a).astype(BF16))
    return [x + _rms(_dot(a, w_down_ref[...]), g_post) for x, a in zip(xs, acts)]


def _stream_cast(srcs, slab, dsts, stage, sem):
    def make_fetch(src, dst):
        rows, cols = dst.shape
        assert rows % STAGE_ROWS == 0 and cols % STAGE_COLS == 0
        ncol = cols // STAGE_COLS
        n = (rows // STAGE_ROWS) * ncol
        assert n % 2 == 0

        def window(k):
            r0 = pl.multiple_of((k // ncol) * STAGE_ROWS, STAGE_ROWS)
            c0 = pl.multiple_of((k % ncol) * STAGE_COLS, STAGE_COLS)
            return pl.ds(r0, STAGE_ROWS), pl.ds(c0, STAGE_COLS)

        def fetch(k):
            rs, cs = window(k)
            return pltpu.make_async_copy(src.at[slab, rs, cs], stage.at[k % 2], sem.at[k % 2])

        return n, window, fetch

    plans = [make_fetch(src, dst) for src, dst in zip(srcs, dsts)]
    plans[0][2](0).start()
    for j, ((n, window, fetch), dst) in enumerate(zip(plans, dsts)):
        next_fetch = plans[j + 1][2] if j + 1 < len(plans) else None

        def body(k, carry, n=n, window=window, fetch=fetch, dst=dst, next_fetch=next_fetch):
            @pl.when(k + 1 < n)
            def _():
                fetch(k + 1).start()

            if next_fetch is not None:
                @pl.when(k + 1 == n)
                def _():
                    next_fetch(0).start()

            fetch(k).wait()
            rs, cs = window(k)
            dst[rs, cs] = stage[k % 2].astype(BF16)
            return carry

        lax.fori_loop(0, n, body, 0)


def _export_copies(bufs, outs, sem):
    return [pltpu.make_async_copy(buf, out, sem.at[j]) for j, (buf, out) in enumerate(zip(bufs, outs))]


def _layer0_prompt_kernel(x_ref, g0, g1, g2, g3, w_grp_ref, scale_ref, conv_w_ref,
                          *rest, layer, n_cast):
    w_hbm, cast_in, rest = rest[:4], rest[4:4 + n_cast], rest[4 + n_cast:]
    (y_ref, pool_out_ref, conv_out_ref), w_exp, rest = rest[:3], rest[3:7], rest[7:]
    cast_out, rest = rest[:n_cast], rest[n_cast:]
    halo_u, halo_cx, w_eff_scr, w_in_ref, w_out_ref, w_up_ref, w_down_ref, stage, sem_in, sem_out = rest
    w_bufs = (w_in_ref, w_out_ref, w_up_ref, w_down_ref)
    tm = x_ref.shape[0]
    t = pl.program_id(1)
    first = (pl.program_id(0) == 0) & (t == 0)
    last = (pl.program_id(0) == pl.num_programs(0) - 1) & (t == pl.num_programs(1) - 1)
    g_pre, g_post, gf_pre, gf_post = _gains((g0, g1, g2, g3), layer)

    for src, dst in zip(cast_in, cast_out):
        dst[...] = src[...].astype(BF16)

    @pl.when(first)
    def _():
        _stream_cast(w_hbm, layer // 2, w_bufs, stage, sem_in)
        for c in _export_copies(w_bufs, w_exp, sem_out):
            c.start()
        _fold_pool_into_out(w_grp_ref, scale_ref, w_out_ref, w_eff_scr)

    @pl.when(last)
    def _():
        for c in _export_copies(w_bufs, w_exp, sem_out):
            c.wait()

    @pl.when(t == 0)
    def _():
        halo_u[...] = jnp.zeros_like(halo_u)
        halo_cx[...] = jnp.zeros_like(halo_cx)

    hist_u, hist_cx = halo_u[...], halo_cx[...]
    for spans in _span_groups(tm):
        xs = [x_ref[pl.ds(r0, ts), :] for r0, ts in spans]
        proj = [_in_proj(x, g_pre, w_in_ref) for x in xs]

        yabs = []
        for (r0, ts), (u, cx, gate_b) in zip(spans, proj):
            full_u = jnp.concatenate([hist_u, u], axis=0)
            full_cx = jnp.concatenate([hist_cx, cx], axis=0)
            hist_u, hist_cx = u[ts - HALO:], cx[ts - HALO:]
            pos = (lax.broadcasted_iota(jnp.int32, (ts, 1), 0) + (t * tm + r0)).astype(F32)
            sums = [s[HALO:] for s in _window_sums(full_u)]
            d = _pool_diff(sums, u, pos)
            cz = _conv_rows(full_cx, conv_w_ref[...])[HALO:]
            yabs.append(jnp.concatenate([d, gate_b * cz], axis=1).astype(BF16))

        x1 = [x + _rms(_dot(yab, w_eff_scr[...]), g_post) for x, yab in zip(xs, yabs)]
        for (r0, ts), y in zip(spans, _ffn_staged(x1, gf_pre, gf_post, w_up_ref, w_down_ref)):
            y_ref[pl.ds(r0, ts), :] = y
    halo_u[...] = hist_u
    halo_cx[...] = hist_cx

    pool_out_ref[...] = halo_u[HALO - POOL_PREV:, :]
    conv_out_ref[...] = halo_cx[HALO - (CONV_W - 1):, :]


def _layer0_sample_kernel(x_ref, pool_ref, conv_ref, g0, g1, g2, g3, w_grp_ref, scale_ref,
                          conv_w_ref, w_in_ref, w_out_ref, w_up_ref, w_down_ref,
                          y_ref, pool_out_ref, conv_out_ref,
                          full_u_scr, full_cx_scr, w_eff_scr, *, layer, seq):
    tm = x_ref.shape[0]
    tb = tm // seq
    grp = HALO + seq
    g_pre, g_post, gf_pre, gf_post = _gains((g0, g1, g2, g3), layer)

    @pl.when(pl.program_id(0) == 0)
    def _():
        _fold_pool_into_out(w_grp_ref, scale_ref, w_out_ref, w_eff_scr)

    ts = tm // SUB_TILES_SAMPLE0
    tbs = tb // SUB_TILES_SAMPLE0
    subs = range(SUB_TILES_SAMPLE0)
    xs = [x_ref[pl.ds(sub * ts, ts), :] for sub in subs]
    proj = [_in_proj(x, g_pre, w_in_ref) for x in xs]

    def new_rows(a):
        return a.reshape(tbs, grp, a.shape[-1])[:, HALO:].reshape(ts, a.shape[-1])

    pos = (lax.broadcasted_iota(jnp.int32, (tbs, seq, 1), 1).reshape(ts, 1) + PAST_LEN).astype(F32)
    yabs = []
    for sub, (u, cx, gate_b) in enumerate(proj):
        sq = pl.ds(sub * tbs, tbs)
        full_u_scr[sq, :HALO - POOL_PREV, :] = jnp.zeros((tbs, HALO - POOL_PREV, D_POOL), F32)
        for r in range(POOL_PREV):
            full_u_scr[sq, HALO - POOL_PREV + r, :] = pool_ref[r, sq, :]
        full_u_scr[sq, HALO:, :] = u.reshape(tbs, seq, D_POOL)
        full_cx_scr[sq, :HALO - (CONV_W - 1), :] = jnp.zeros((tbs, HALO - (CONV_W - 1), D_CONV), F32)
        full_cx_scr[sq, HALO - (CONV_W - 1):HALO, :] = conv_ref[sq]
        full_cx_scr[sq, HALO:, :] = cx.reshape(tbs, seq, D_CONV)
        for r in range(POOL_PREV):
            pool_out_ref[r, sq, :] = full_u_scr[sq, grp - POOL_PREV + r, :]
        conv_out_ref[sq] = full_cx_scr[sq, grp - (CONV_W - 1):, :]
        full_u = full_u_scr[sq].reshape(tbs * grp, D_POOL)
        full_cx = full_cx_scr[sq].reshape(tbs * grp, D_CONV)
        sums = [new_rows(s) for s in _window_sums(full_u)]
        d = _pool_diff(sums, u, pos)
        cz = new_rows(_conv_rows(full_cx, conv_w_ref[...]))
        yabs.append(jnp.concatenate([d, gate_b * cz], axis=1).astype(BF16))

    x1 = [x + _rms(_dot(yab, w_eff_scr[...]), g_post) for x, yab in zip(xs, yabs)]
    for sub, y in enumerate(_ffn_staged(x1, gf_pre, gf_post, w_up_ref, w_down_ref)):
        y_ref[pl.ds(sub * ts, ts), :] = y


def _sgu_tables(w_sp_ref, b_sp_ref, wm_scr, bias_scr, seq):
    r = lax.broadcasted_iota(jnp.int32, (CHUNK, CHUNK), 0)
    c = lax.broadcasted_iota(jnp.int32, (CHUNK, CHUNK), 1)
    keep = r >= c
    b = b_sp_ref[...]
    if seq is not None:
        keep = keep & (r // seq == c // seq)
        pick = (r % seq == c).astype(BF16)
        pick_t = (c % seq == r).astype(BF16)
        b = jnp.where(lax.broadcasted_iota(jnp.int32, b.shape, 1) < seq, b, 0.0)
        k = seq
        while k < CHUNK:
            b = b + pltpu.roll(b, k, axis=1)
            k *= 2
    mask = keep.astype(F32)
    for hh in range(SGU_HEADS):
        w = w_sp_ref[hh]
        if seq is not None:
            w = _dot(_dot(pick, w.astype(BF16)).astype(BF16), pick_t)
        wm_scr[hh] = (w * mask).astype(BF16)
        rows = jnp.broadcast_to(b[hh:hh + 1, :], (CHUNK, SGU_HD))
        bias_scr[:, hh * SGU_HD:(hh + 1) * SGU_HD] = rows.T


def _layer1_kernel(x_ref, g0, g1, g2, g3, g_v_ref, w_sp_ref, b_sp_ref,
                   w_uv_ref, w_out_ref, w_up_ref, w_down_ref, y_ref, *rest, layer, seq, emit_v):
    if emit_v:
        v_ref, wm_scr, bias_scr = rest
    else:
        wm_scr, bias_scr = rest
    tm = x_ref.shape[0]
    g_pre, g_post, gf_pre, gf_post = _gains((g0, g1, g2, g3), layer)

    @pl.when(pl.program_id(0) == 0)
    def _():
        _sgu_tables(w_sp_ref, b_sp_ref, wm_scr, bias_scr, seq)

    bias = bias_scr[...]
    for spans in _span_groups(tm):
        xs = [x_ref[pl.ds(r0, ts), :] for r0, ts in spans]
        ps = [_dot(_rms(x, g_pre).astype(BF16), w_uv_ref[...]) for x in xs]

        gated = []
        for (r0, ts), p in zip(spans, ps):
            nc = ts // CHUNK
            u = p[:, :D_SGU]
            v = _rms(p[:, D_SGU:], g_v_ref[...])
            if emit_v:
                v_ref[pl.ds(r0, ts), :] = v
            vb = v.astype(BF16)
            cols = []
            for hh in range(SGU_HEADS):
                lanes = slice(hh * SGU_HD, (hh + 1) * SGU_HD)
                rhs = jnp.concatenate(
                    [vb[ci * CHUNK:(ci + 1) * CHUNK, lanes] for ci in range(nc)], axis=1)
                out = _dot(wm_scr[hh], rhs)
                cols.append(jnp.concatenate(
                    [out[:, ci * SGU_HD:(ci + 1) * SGU_HD] + bias[:, lanes] for ci in range(nc)],
                    axis=0))
            gated.append((u * jnp.concatenate(cols, axis=1)).astype(BF16))

        x1 = [x + _rms(_dot(a, w_out_ref[...]), g_post) for x, a in zip(xs, gated)]
        for (r0, ts), y in zip(spans, _ffn_staged(x1, gf_pre, gf_post, w_up_ref, w_down_ref)):
            y_ref[pl.ds(r0, ts), :] = y


def _resident(arr, layer=None):
    if layer is None:
        nd = arr.ndim
        return pl.BlockSpec(arr.shape, lambda *_: (0,) * nd, pipeline_mode=pl.Buffered(1))
    nd = arr.ndim - 1
    return pl.BlockSpec((None,) + arr.shape[1:], lambda *_: (layer,) + (0,) * nd,
                        pipeline_mode=pl.Buffered(1))


def _params(n_axes):
    return pltpu.CompilerParams(dimension_semantics=("arbitrary",) * n_axes,
                                vmem_limit_bytes=VMEM_LIMIT_BYTES)


def _split(weights):
    return [a for a, _ in weights], [_resident(a, l) for a, l in weights]


def _layer0_prompt(x, weights, big, layer, to_cast):
    b, s, d = x.shape
    tm = TILE_M_PROMPT
    nt = s // tm
    steps = b * nt
    arrs, specs = _split(weights)
    any_spec = pl.BlockSpec(memory_space=pl.ANY)
    big_shapes = [w.shape[1:] for w in big]
    tile = pl.BlockSpec((None, tm, d), lambda i, j: (i, j, 0))
    pool = pl.BlockSpec((None, None, POOL_PREV, D_POOL), lambda i, j: (0, i, 0, 0))
    conv = pl.BlockSpec((None, None, CONV_W - 1, D_CONV), lambda i, j: (0, i, 0, 0))
    cast_in, cast_out, cast_shapes = [], [], []
    for w, slab in to_cast:
        _, rows, cols = w.shape
        assert rows % steps == 0
        cast_in.append(pl.BlockSpec((None, rows // steps, cols),
                                    lambda i, j, slab=slab: (slab, i * nt + j, 0)))
        cast_out.append(pl.BlockSpec((rows // steps, cols), lambda i, j: (i * nt + j, 0)))
        cast_shapes.append(jax.ShapeDtypeStruct((rows, cols), BF16))
    return pl.pallas_call(
        functools.partial(_layer0_prompt_kernel, layer=layer, n_cast=len(to_cast)),
        grid=(b, nt),
        in_specs=[tile] + specs + [any_spec] * len(big) + cast_in,
        out_specs=[tile, pool, conv] + [any_spec] * len(big) + cast_out,
        out_shape=[jax.ShapeDtypeStruct(x.shape, F32),
                   jax.ShapeDtypeStruct((1, b, POOL_PREV, D_POOL), F32),
                   jax.ShapeDtypeStruct((1, b, CONV_W - 1, D_CONV), F32)]
        + [jax.ShapeDtypeStruct(sh, BF16) for sh in big_shapes] + cast_shapes,
        scratch_shapes=[pltpu.VMEM((HALO, D_POOL), F32), pltpu.VMEM((HALO, D_CONV), F32),
                        pltpu.VMEM((D_POOL + D_CONV, d), BF16)]
        + [pltpu.VMEM(sh, BF16) for sh in big_shapes]
        + [pltpu.VMEM((2, STAGE_ROWS, STAGE_COLS), F32), pltpu.SemaphoreType.DMA((2,)),
           pltpu.SemaphoreType.DMA((len(big),))],
        compiler_params=_params(2),
        name="layer0_prompt",
    )(x, *arrs, *big, *[w for w, _ in to_cast])


def _layer0_sample(x, state_pool, state_conv, weights, layer, seq):
    n, d = x.shape
    tm = TILE_M_SAMPLE
    tb = tm // seq
    arrs, specs = _split(weights)
    tile = pl.BlockSpec((tm, d), lambda i: (i, 0))
    pool_t = jnp.swapaxes(state_pool, 1, 2)
    pool = pl.BlockSpec((None, POOL_PREV, tb, D_POOL), lambda i: (0, 0, i, 0))
    conv = pl.BlockSpec((None, tb, CONV_W - 1, D_CONV), lambda i: (0, i, 0, 0))
    y, pool_new, conv_new = pl.pallas_call(
        functools.partial(_layer0_sample_kernel, layer=layer, seq=seq),
        grid=(n // tm,),
        in_specs=[tile, pool, conv] + specs,
        out_specs=[tile, pool, conv],
        out_shape=[jax.ShapeDtypeStruct(x.shape, F32),
                   jax.ShapeDtypeStruct(pool_t.shape, F32),
                   jax.ShapeDtypeStruct(state_conv.shape, F32)],
        scratch_shapes=[pltpu.VMEM((tb, HALO + seq, D_POOL), F32),
                        pltpu.VMEM((tb, HALO + seq, D_CONV), F32),
                        pltpu.VMEM((D_POOL + D_CONV, d), BF16)],
        compiler_params=_params(1),
        name="layer0_sample",
    )(x, pool_t, state_conv, *arrs)
    return y, jnp.swapaxes(pool_new, 1, 2), conv_new


def _layer1(x, weights, tm, layer, seq, emit_v, name):
    n, d = x.shape
    arrs, specs = _split(weights)
    tile = pl.BlockSpec((tm, d), lambda i: (i, 0))
    out_specs = [tile, tile] if emit_v else [tile]
    out_shape = [jax.ShapeDtypeStruct(x.shape, F32)] * len(out_specs)
    return pl.pallas_call(
        functools.partial(_layer1_kernel, layer=layer, seq=seq, emit_v=emit_v),
        grid=(n // tm,),
        in_specs=[tile] + specs,
        out_specs=out_specs,
        out_shape=out_shape,
        scratch_shapes=[pltpu.VMEM((SGU_HEADS, CHUNK, CHUNK), BF16),
                        pltpu.VMEM((CHUNK, D_SGU), F32)],
        compiler_params=_params(1),
        name=name,
    )(x, *arrs)


def kernel(x_prompt, x_sample, state_pool, state_conv, g_mix_pre, g_mix_post, g_ffn_pre,
           g_ffn_post, w_in_ab, w_pool_grp, pool_scale, conv_w, w_out_ab, w_uv, g_v, w_spatial,
           b_spatial, w_out_c, w_up, w_down):
    bp, sp, d = x_prompt.shape
    bs, ss, _ = x_sample.shape
    assert ss < CHUNK and CHUNK % ss == 0 and CONV_W - 1 <= ss and sp % CHUNK == 0

    gains = ((g_mix_pre, None), (g_mix_post, None), (g_ffn_pre, None), (g_ffn_post, None))
    w0_small = gains + ((w_pool_grp, 0), (pool_scale, None), (conv_w, 0))

    (y0, pool_p, conv_p, w_in_b, w_out_b, w_up0_b, w_down0_b) = _layer0_prompt(
        x_prompt, w0_small, (w_in_ab, w_out_ab, w_up, w_down), 0, ())
    w_uv_b, w_out_c_b = w_uv[0].astype(BF16), w_out_c[0].astype(BF16)
    w_up1_b, w_down1_b = w_up[1].astype(BF16), w_down[1].astype(BF16)
    w0 = w0_small + ((w_in_b, None), (w_out_b, None), (w_up0_b, None), (w_down0_b, None))
    w1 = gains + ((g_v, None), (w_spatial, 0), (b_spatial, 0),
                  (w_uv_b, None), (w_out_c_b, None), (w_up1_b, None), (w_down1_b, None))
    (y_prompt,) = _layer1(y0.reshape(bp * sp, d), w1, TILE_M_PROMPT, 1, None, False,
                          "layer1_prompt")

    ys0, pool_s, conv_s = _layer0_sample(x_sample.reshape(bs * ss, d), state_pool, state_conv,
                                         w0, 0, ss)
    ys1, v_s = _layer1(ys0, w1, TILE_M_SAMPLE, 1, ss, True, "layer1_sample")
    return (y_prompt.reshape(bp, sp, d), ys1.reshape(bs, ss, d), pool_p, pool_s, conv_p, conv_s,
            v_s.reshape(1, bs, ss, D_SGU))
```

```python
import functools

import jax
import jax.numpy as jnp
from jax import lax
from jax.experimental import pallas as pl
from jax.experimental.pallas import tpu as pltpu

D_MODEL = 1024
EPS = 1e-6
D_POOL = D_MODEL // 2
POOL_WINDOWS = (2, 4, 8, 16)
POOL_GC = D_POOL // len(POOL_WINDOWS)
POOL_PREV = max(POOL_WINDOWS) - 1
D_CONV = D_MODEL // 2
CONV_W = 3
D_SGU = D_MODEL
SGU_HEADS = 8
SGU_HD = D_SGU // SGU_HEADS
CHUNK = 128
PAST_LEN = 16384

HALO = 16
TILE_M_PROMPT = 512
TILE_M_SAMPLE = 512
SUB_TILE_ROWS = 256
SUB_GROUP = 2
SUB_TILES_SAMPLE0 = 1
STAGE_ROWS, STAGE_COLS = 512, 1024
STAGE_SLOTS = 3
VMEM_LIMIT_BYTES = 60 * 1024 * 1024

F32 = jnp.float32
BF16 = jnp.bfloat16


def _rms(x, g):
    return x * lax.rsqrt(jnp.mean(x * x, axis=-1, keepdims=True) + EPS) * g


def _dot(a, b):
    return jnp.dot(a, b, preferred_element_type=F32)


def _shift_rows(a, k):
    return pltpu.roll(a, k, axis=0)


def _window_sums(full_u):
    sums = []
    for g, w in enumerate(POOL_WINDOWS):
        s = full_u[:, g * POOL_GC:(g + 1) * POOL_GC]
        k = 1
        while k < w:
            s = s + _shift_rows(s, k)
            k *= 2
        sums.append(s)
    return sums


def _pool_diff(sums, u, pos):
    outs = []
    for g, w in enumerate(POOL_WINDOWS):
        inv_cnt = 1.0 / jnp.minimum(pos + 1.0, float(w))
        outs.append(sums[g] * inv_cnt - u[:, g * POOL_GC:(g + 1) * POOL_GC])
    return jnp.concatenate(outs, axis=1)


def _fold_pool_into_out(w_grp_ref, scale_ref, w_out_ref, w_eff_scr):
    scale = scale_ref[...]
    for g in range(len(POOL_WINDOWS)):
        rows = slice(g * POOL_GC, (g + 1) * POOL_GC)
        a = w_grp_ref[g] * scale[:, rows]
        a_hi = a.astype(BF16)
        a_lo = (a - a_hi.astype(F32)).astype(BF16)
        b = w_out_ref[rows, :]
        w_eff_scr[rows, :] = (_dot(a_hi, b) + _dot(a_lo, b)).astype(BF16)
    w_eff_scr[D_POOL:, :] = w_out_ref[D_POOL:, :]


def _conv_rows(full_cx, conv_w):
    return (conv_w[0:1] * _shift_rows(full_cx, 2) + conv_w[1:2] * _shift_rows(full_cx, 1)
            + conv_w[2:3] * full_cx)


def _gains(gain_refs, layer):
    return [r[layer:layer + 1, :] for r in gain_refs]


def _in_proj(x, g_pre, w_in_ref):
    p = _dot(_rms(x, g_pre).astype(BF16), w_in_ref[...])
    u = p[:, :D_POOL]
    cx = p[:, D_POOL + 2 * D_CONV:] * p[:, D_POOL:D_POOL + D_CONV]
    gate_b = p[:, D_POOL + D_CONV:D_POOL + 2 * D_CONV]
    return u, cx, gate_b


def _ffn(x, g_pre, g_post, w_up_ref, w_down_ref):
    h = _rms(x, g_pre).astype(BF16)
    a = jnp.maximum(_dot(h, w_up_ref[...]), 0.0)
    a = (a * a).astype(BF16)
    f = _dot(a, w_down_ref[...])
    return x + _rms(f, g_post)


def _sub_spans(tm):
    assert tm % SUB_TILE_ROWS == 0
    return [(r0, SUB_TILE_ROWS) for r0 in range(0, tm, SUB_TILE_ROWS)]


def _span_groups(tm):
    spans = _sub_spans(tm)
    return [spans[k:k + SUB_GROUP] for k in range(0, len(spans), SUB_GROUP)]


def _ffn_staged(xs, g_pre, g_post, w_up_ref, w_down_ref):
    acts = []
    for x in xs:
        a = jnp.maximum(_dot(_rms(x, g_pre).astype(BF16), w_up_ref[...]), 0.0)
        acts.append((a * a).astype(BF16))
    return [x + _rms(_dot(a, w_down_ref[...]), g_post) for x, a in zip(xs, acts)]


def _stream_cast(srcs, slab, dsts, stage, sem):
    ahead = STAGE_SLOTS - 1

    def make_plan(src, dst, first):
        rows, cols = dst.shape
        assert rows % STAGE_ROWS == 0 and cols % STAGE_COLS == 0
        ncol = cols // STAGE_COLS
        n = (rows // STAGE_ROWS) * ncol
        assert n >= ahead

        def window(k):
            r0 = pl.multiple_of((k // ncol) * STAGE_ROWS, STAGE_ROWS)
            c0 = pl.multiple_of((k % ncol) * STAGE_COLS, STAGE_COLS)
            return pl.ds(r0, STAGE_ROWS), pl.ds(c0, STAGE_COLS)

        def fetch(k):
            rs, cs = window(k)
            slot = (first + k) % STAGE_SLOTS
            return pltpu.make_async_copy(src.at[slab, rs, cs], stage.at[slot], sem.at[slot])

        return n, window, fetch

    plans, first = [], 0
    for src, dst in zip(srcs, dsts):
        plans.append(make_plan(src, dst, first) + (first,))
        first += plans[-1][0]

    for k in range(ahead):
        plans[0][2](k).start()
    for j, ((n, window, fetch, first), dst) in enumerate(zip(plans, dsts)):
        next_fetch = plans[j + 1][2] if j + 1 < len(plans) else None

        def body(k, carry, n=n, window=window, fetch=fetch, first=first, dst=dst,
                 next_fetch=next_fetch):
            @pl.when(k + ahead < n)
            def _():
                fetch(k + ahead).start()

            if next_fetch is not None:
                @pl.when(k + ahead >= n)
                def _():
                    next_fetch(k + ahead - n).start()

            fetch(k).wait()
            rs, cs = window(k)
            dst[rs, cs] = stage[(first + k) % STAGE_SLOTS].astype(BF16)
            return carry

        lax.fori_loop(0, n, body, 0)


def _export_copies(bufs, outs, sem):
    return [pltpu.make_async_copy(buf, out, sem.at[j]) for j, (buf, out) in enumerate(zip(bufs, outs))]


def _layer0_prompt_kernel(x_ref, g0, g1, g2, g3, w_grp_ref, scale_ref, conv_w_ref,
                          *rest, layer, n_cast):
    w_hbm, cast_in, rest = rest[:4], rest[4:4 + n_cast], rest[4 + n_cast:]
    (y_ref, pool_out_ref, conv_out_ref), w_exp, rest = rest[:3], rest[3:7], rest[7:]
    cast_out, rest = rest[:n_cast], rest[n_cast:]
    halo_u, halo_cx, w_eff_scr, w_in_ref, w_out_ref, w_up_ref, w_down_ref, stage, sem_in, sem_out = rest
    w_bufs = (w_in_ref, w_out_ref, w_up_ref, w_down_ref)
    tm = x_ref.shape[0]
    t = pl.program_id(1)
    first = (pl.program_id(0) == 0) & (t == 0)
    last = (pl.program_id(0) == pl.num_programs(0) - 1) & (t == pl.num_programs(1) - 1)
    g_pre, g_post, gf_pre, gf_post = _gains((g0, g1, g2, g3), layer)

    for src, dst in zip(cast_in, cast_out):
        dst[...] = src[...].astype(BF16)

    @pl.when(first)
    def _():
        _stream_cast(w_hbm, layer // 2, w_bufs, stage, sem_in)
        for c in _export_copies(w_bufs, w_exp, sem_out):
            c.start()
        _fold_pool_into_out(w_grp_ref, scale_ref, w_out_ref, w_eff_scr)

    @pl.when(last)
    def _():
        for c in _export_copies(w_bufs, w_exp, sem_out):
            c.wait()

    @pl.when(t == 0)
    def _():
        halo_u[...] = jnp.zeros_like(halo_u)
        halo_cx[...] = jnp.zeros_like(halo_cx)

    hist_u, hist_cx = halo_u[...], halo_cx[...]
    for spans in _span_groups(tm):
        xs = [x_ref[pl.ds(r0, ts), :] for r0, ts in spans]
        proj = [_in_proj(x, g_pre, w_in_ref) for x in xs]

        yabs = []
        for (r0, ts), (u, cx, gate_b) in zip(spans, proj):
            full_u = jnp.concatenate([hist_u, u], axis=0)
            full_cx = jnp.concatenate([hist_cx, cx], axis=0)
            hist_u, hist_cx = u[ts - HALO:], cx[ts - HALO:]
            pos = (lax.broadcasted_iota(jnp.int32, (ts, 1), 0) + (t * tm + r0)).astype(F32)
            sums = [s[HALO:] for s in _window_sums(full_u)]
            d = _pool_diff(sums, u, pos)
            cz = _conv_rows(full_cx, conv_w_ref[...])[HALO:]
            yabs.append(jnp.concatenate([d, gate_b * cz], axis=1).astype(BF16))

        x1 = [x + _rms(_dot(yab, w_eff_scr[...]), g_post) for x, yab in zip(xs, yabs)]
        for (r0, ts), y in zip(spans, _ffn_staged(x1, gf_pre, gf_post, w_up_ref, w_down_ref)):
            y_ref[pl.ds(r0, ts), :] = y
    halo_u[...] = hist_u
    halo_cx[...] = hist_cx

    pool_out_ref[...] = halo_u[HALO - POOL_PREV:, :]
    conv_out_ref[...] = halo_cx[HALO - (CONV_W - 1):, :]


def _layer0_sample_kernel(x_ref, pool_ref, conv_ref, g0, g1, g2, g3, w_grp_ref, scale_ref,
                          conv_w_ref, w_in_ref, w_out_ref, w_up_ref, w_down_ref,
                          y_ref, pool_out_ref, conv_out_ref,
                          full_u_scr, full_cx_scr, w_eff_scr, *, layer, seq):
    tm = x_ref.shape[0]
    tb = tm // seq
    grp = HALO + seq
    g_pre, g_post, gf_pre, gf_post = _gains((g0, g1, g2, g3), layer)

    @pl.when(pl.program_id(0) == 0)
    def _():
        _fold_pool_into_out(w_grp_ref, scale_ref, w_out_ref, w_eff_scr)

    ts = tm // SUB_TILES_SAMPLE0
    tbs = tb // SUB_TILES_SAMPLE0
    subs = range(SUB_TILES_SAMPLE0)
    xs = [x_ref[pl.ds(sub * ts, ts), :] for sub in subs]
    proj = [_in_proj(x, g_pre, w_in_ref) for x in xs]

    def new_rows(a):
        return a.reshape(tbs, grp, a.shape[-1])[:, HALO:].reshape(ts, a.shape[-1])

    pos = (lax.broadcasted_iota(jnp.int32, (tbs, seq, 1), 1).reshape(ts, 1) + PAST_LEN).astype(F32)
    yabs = []
    for sub, (u, cx, gate_b) in enumerate(proj):
        sq = pl.ds(sub * tbs, tbs)
        full_u_scr[sq, :HALO - POOL_PREV, :] = jnp.zeros((tbs, HALO - POOL_PREV, D_POOL), F32)
        for r in range(POOL_PREV):
            full_u_scr[sq, HALO - POOL_PREV + r, :] = pool_ref[r, sq, :]
        full_u_scr[sq, HALO:, :] = u.reshape(tbs, seq, D_POOL)
        full_cx_scr[sq, :HALO - (CONV_W - 1), :] = jnp.zeros((tbs, HALO - (CONV_W - 1), D_CONV), F32)
        full_cx_scr[sq, HALO - (CONV_W - 1):HALO, :] = conv_ref[sq]
        full_cx_scr[sq, HALO:, :] = cx.reshape(tbs, seq, D_CONV)
        for r in range(POOL_PREV):
            pool_out_ref[r, sq, :] = full_u_scr[sq, grp - POOL_PREV + r, :]
        conv_out_ref[sq] = full_cx_scr[sq, grp - (CONV_W - 1):, :]
        full_u = full_u_scr[sq].reshape(tbs * grp, D_POOL)
        full_cx = full_cx_scr[sq].reshape(tbs * grp, D_CONV)
        sums = [new_rows(s) for s in _window_sums(full_u)]
        d = _pool_diff(sums, u, pos)
        cz = new_rows(_conv_rows(full_cx, conv_w_ref[...]))
        yabs.append(jnp.concatenate([d, gate_b * cz], axis=1).astype(BF16))

    x1 = [x + _rms(_dot(yab, w_eff_scr[...]), g_post) for x, yab in zip(xs, yabs)]
    for sub, y in enumerate(_ffn_staged(x1, gf_pre, gf_post, w_up_ref, w_down_ref)):
        y_ref[pl.ds(sub * ts, ts), :] = y


def _sgu_tables(w_sp_ref, b_sp_ref, wm_scr, bias_scr, seq):
    r = lax.broadcasted_iota(jnp.int32, (CHUNK, CHUNK), 0)
    c = lax.broadcasted_iota(jnp.int32, (CHUNK, CHUNK), 1)
    keep = r >= c
    b = b_sp_ref[...]
    if seq is not None:
        keep = keep & (r // seq == c // seq)
        pick = (r % seq == c).astype(BF16)
        pick_t = (c % seq == r).astype(BF16)
        b = jnp.where(lax.broadcasted_iota(jnp.int32, b.shape, 1) < seq, b, 0.0)
        k = seq
        while k < CHUNK:
            b = b + pltpu.roll(b, k, axis=1)
            k *= 2
    mask = keep.astype(F32)
    for hh in range(SGU_HEADS):
        w = w_sp_ref[hh]
        if seq is not None:
            w = _dot(_dot(pick, w.astype(BF16)).astype(BF16), pick_t)
        wm_scr[hh] = (w * mask).astype(BF16)
        rows = jnp.broadcast_to(b[hh:hh + 1, :], (CHUNK, SGU_HD))
        bias_scr[:, hh * SGU_HD:(hh + 1) * SGU_HD] = rows.T


def _layer1_kernel(x_ref, g0, g1, g2, g3, g_v_ref, w_sp_ref, b_sp_ref,
                   w_uv_ref, w_out_ref, w_up_ref, w_down_ref, y_ref, *rest, layer, seq, emit_v):
    if emit_v:
        v_ref, wm_scr, bias_scr = rest
    else:
        wm_scr, bias_scr = rest
    tm = x_ref.shape[0]
    g_pre, g_post, gf_pre, gf_post = _gains((g0, g1, g2, g3), layer)

    @pl.when(pl.program_id(0) == 0)
    def _():
        _sgu_tables(w_sp_ref, b_sp_ref, wm_scr, bias_scr, seq)

    bias = bias_scr[...]
    for spans in _span_groups(tm):
        xs = [x_ref[pl.ds(r0, ts), :] for r0, ts in spans]
        ps = [_dot(_rms(x, g_pre).astype(BF16), w_uv_ref[...]) for x in xs]

        gated = []
        for (r0, ts), p in zip(spans, ps):
            nc = ts // CHUNK
            u = p[:, :D_SGU]
            v = _rms(p[:, D_SGU:], g_v_ref[...])
            if emit_v:
                v_ref[pl.ds(r0, ts), :] = v
            vb = v.astype(BF16)
            cols = []
            for hh in range(SGU_HEADS):
                lanes = slice(hh * SGU_HD, (hh + 1) * SGU_HD)
                rhs = jnp.concatenate(
                    [vb[ci * CHUNK:(ci + 1) * CHUNK, lanes] for ci in range(nc)], axis=1)
                out = _dot(wm_scr[hh], rhs)
                cols.append(jnp.concatenate(
                    [out[:, ci * SGU_HD:(ci + 1) * SGU_HD] + bias[:, lanes] for ci in range(nc)],
                    axis=0))
            gated.append((u * jnp.concatenate(cols, axis=1)).astype(BF16))

        x1 = [x + _rms(_dot(a, w_out_ref[...]), g_post) for x, a in zip(xs, gated)]
        for (r0, ts), y in zip(spans, _ffn_staged(x1, gf_pre, gf_post, w_up_ref, w_down_ref)):
            y_ref[pl.ds(r0, ts), :] = y


def _resident(arr, layer=None):
    if layer is None:
        nd = arr.ndim
        return pl.BlockSpec(arr.shape, lambda *_: (0,) * nd, pipeline_mode=pl.Buffered(1))
    nd = arr.ndim - 1
    return pl.BlockSpec((None,) + arr.shape[1:], lambda *_: (layer,) + (0,) * nd,
                        pipeline_mode=pl.Buffered(1))


def _params(n_axes):
    return pltpu.CompilerParams(dimension_semantics=("arbitrary",) * n_axes,
                                vmem_limit_bytes=VMEM_LIMIT_BYTES)


def _split(weights):
    return [a for a, _ in weights], [_resident(a, l) for a, l in weights]


def _layer0_prompt(x, weights, big, layer, to_cast):
    b, s, d = x.shape
    tm = TILE_M_PROMPT
    nt = s // tm
    steps = b * nt
    arrs, specs = _split(weights)
    any_spec = pl.BlockSpec(memory_space=pl.ANY)
    big_shapes = [w.shape[1:] for w in big]
    tile = pl.BlockSpec((None, tm, d), lambda i, j: (i, j, 0))
    pool = pl.BlockSpec((None, None, POOL_PREV, D_POOL), lambda i, j: (0, i, 0, 0))
    conv = pl.BlockSpec((None, None, CONV_W - 1, D_CONV), lambda i, j: (0, i, 0, 0))
    cast_in, cast_out, cast_shapes = [], [], []
    for w, slab in to_cast:
        _, rows, cols = w.shape
        assert rows % steps == 0
        cast_in.append(pl.BlockSpec((None, rows // steps, cols),
                                    lambda i, j, slab=slab: (slab, i * nt + j, 0)))
        cast_out.append(pl.BlockSpec((rows // steps, cols), lambda i, j: (i * nt + j, 0)))
        cast_shapes.append(jax.ShapeDtypeStruct((rows, cols), BF16))
    return pl.pallas_call(
        functools.partial(_layer0_prompt_kernel, layer=layer, n_cast=len(to_cast)),
        grid=(b, nt),
        in_specs=[tile] + specs + [any_spec] * len(big) + cast_in,
        out_specs=[tile, pool, conv] + [any_spec] * len(big) + cast_out,
        out_shape=[jax.ShapeDtypeStruct(x.shape, F32),
                   jax.ShapeDtypeStruct((1, b, POOL_PREV, D_POOL), F32),
                   jax.ShapeDtypeStruct((1, b, CONV_W - 1, D_CONV), F32)]
        + [jax.ShapeDtypeStruct(sh, BF16) for sh in big_shapes] + cast_shapes,
        scratch_shapes=[pltpu.VMEM((HALO, D_POOL), F32), pltpu.VMEM((HALO, D_CONV), F32),
                        pltpu.VMEM((D_POOL + D_CONV, d), BF16)]
        + [pltpu.VMEM(sh, BF16) for sh in big_shapes]
        + [pltpu.VMEM((STAGE_SLOTS, STAGE_ROWS, STAGE_COLS), F32),
           pltpu.SemaphoreType.DMA((STAGE_SLOTS,)),
           pltpu.SemaphoreType.DMA((len(big),))],
        compiler_params=_params(2),
        name="layer0_prompt",
    )(x, *arrs, *big, *[w for w, _ in to_cast])


def _layer0_sample(x, state_pool, state_conv, weights, layer, seq):
    n, d = x.shape
    tm = TILE_M_SAMPLE
    tb = tm // seq
    arrs, specs = _split(weights)
    tile = pl.BlockSpec((tm, d), lambda i: (i, 0))
    pool_t = jnp.swapaxes(state_pool, 1, 2)
    pool = pl.BlockSpec((None, POOL_PREV, tb, D_POOL), lambda i: (0, 0, i, 0))
    conv = pl.BlockSpec((None, tb, CONV_W - 1, D_CONV), lambda i: (0, i, 0, 0))
    y, pool_new, conv_new = pl.pallas_call(
        functools.partial(_layer0_sample_kernel, layer=layer, seq=seq),
        grid=(n // tm,),
        in_specs=[tile, pool, conv] + specs,
        out_specs=[tile, pool, conv],
        out_shape=[jax.ShapeDtypeStruct(x.shape, F32),
                   jax.ShapeDtypeStruct(pool_t.shape, F32),
                   jax.ShapeDtypeStruct(state_conv.shape, F32)],
        scratch_shapes=[pltpu.VMEM((tb, HALO + seq, D_POOL), F32),
                        pltpu.VMEM((tb, HALO + seq, D_CONV), F32),
                        pltpu.VMEM((D_POOL + D_CONV, d), BF16)],
        compiler_params=_params(1),
        name="layer0_sample",
    )(x, pool_t, state_conv, *arrs)
    return y, jnp.swapaxes(pool_new, 1, 2), conv_new


def _layer1(x, weights, tm, layer, seq, emit_v, name):
    n, d = x.shape
    arrs, specs = _split(weights)
    tile = pl.BlockSpec((tm, d), lambda i: (i, 0))
    out_specs = [tile, tile] if emit_v else [tile]
    out_shape = [jax.ShapeDtypeStruct(x.shape, F32)] * len(out_specs)
    return pl.pallas_call(
        functools.partial(_layer1_kernel, layer=layer, seq=seq, emit_v=emit_v),
        grid=(n // tm,),
        in_specs=[tile] + specs,
        out_specs=out_specs,
        out_shape=out_shape,
        scratch_shapes=[pltpu.VMEM((SGU_HEADS, CHUNK, CHUNK), BF16),
                        pltpu.VMEM((CHUNK, D_SGU), F32)],
        compiler_params=_params(1),
        name=name,
    )(x, *arrs)


def kernel(x_prompt, x_sample, state_pool, state_conv, g_mix_pre, g_mix_post, g_ffn_pre,
           g_ffn_post, w_in_ab, w_pool_grp, pool_scale, conv_w, w_out_ab, w_uv, g_v, w_spatial,
           b_spatial, w_out_c, w_up, w_down):
    bp, sp, d = x_prompt.shape
    bs, ss, _ = x_sample.shape
    assert ss < CHUNK and CHUNK % ss == 0 and CONV_W - 1 <= ss and sp % CHUNK == 0

    gains = ((g_mix_pre, None), (g_mix_post, None), (g_ffn_pre, None), (g_ffn_post, None))
    w0_small = gains + ((w_pool_grp, 0), (pool_scale, None), (conv_w, 0))

    (y0, pool_p, conv_p, w_in_b, w_out_b, w_up0_b, w_down0_b,
     w_uv_b, w_out_c_b, w_up1_b, w_down1_b) = _layer0_prompt(
        x_prompt, w0_small, (w_in_ab, w_out_ab, w_up, w_down), 0,
        ((w_uv, 0), (w_out_c, 0), (w_up, 1), (w_down, 1)))
    w0 = w0_small + ((w_in_b, None), (w_out_b, None), (w_up0_b, None), (w_down0_b, None))
    w1 = gains + ((g_v, None), (w_spatial, 0), (b_spatial, 0),
                  (w_uv_b, None), (w_out_c_b, None), (w_up1_b, None), (w_down1_b, None))
    (y_prompt,) = _layer1(y0.reshape(bp * sp, d), w1, TILE_M_PROMPT, 1, None, False,
                          "layer1_prompt")

    ys0, pool_s, conv_s = _layer0_sample(x_sample.reshape(bs * ss, d), state_pool, state_conv,
                                         w0, 0, ss)
    ys1, v_s = _layer1(ys0, w1, TILE_M_SAMPLE, 1, ss, True, "layer1_sample")
    return (y_prompt.reshape(bp, sp, d), ys1.reshape(bs, ss, d), pool_p, pool_s, conv_p, conv_s,
            v_s.reshape(1, bs, ss, D_SGU))
```

```python
import functools

import jax
import jax.numpy as jnp
from jax import lax
from jax.experimental import pallas as pl
from jax.experimental.pallas import tpu as pltpu

D_MODEL = 1024
EPS = 1e-6
D_POOL = D_MODEL // 2
POOL_WINDOWS = (2, 4, 8, 16)
POOL_GC = D_POOL // len(POOL_WINDOWS)
POOL_PREV = max(POOL_WINDOWS) - 1
D_CONV = D_MODEL // 2
CONV_W = 3
D_SGU = D_MODEL
SGU_HEADS = 8
SGU_HD = D_SGU // SGU_HEADS
CHUNK = 128
PAST_LEN = 16384

HALO = 16
TILE_M_PROMPT = 512
TILE_M_SAMPLE = 512
SUB_TILE_ROWS = 256
SUB_GROUP = 2
SUB_TILES_SAMPLE0 = 1
STAGE_ROWS, STAGE_COLS = 512, 1024
STAGE_SLOTS = 3
VMEM_LIMIT_BYTES = 60 * 1024 * 1024

F32 = jnp.float32
BF16 = jnp.bfloat16


def _rms(x, g):
    return x * lax.rsqrt(jnp.mean(x * x, axis=-1, keepdims=True) + EPS) * g


def _dot(a, b):
    return jnp.dot(a, b, preferred_element_type=F32)


def _shift_rows(a, k):
    return pltpu.roll(a, k, axis=0)


def _window_sums(full_u):
    sums = []
    for g, w in enumerate(POOL_WINDOWS):
        s = full_u[:, g * POOL_GC:(g + 1) * POOL_GC]
        k = 1
        while k < w:
            s = s + _shift_rows(s, k)
            k *= 2
        sums.append(s)
    return sums


def _pool_diff(sums, u, pos):
    outs = []
    for g, w in enumerate(POOL_WINDOWS):
        inv_cnt = 1.0 / jnp.minimum(pos + 1.0, float(w))
        outs.append(sums[g] * inv_cnt - u[:, g * POOL_GC:(g + 1) * POOL_GC])
    return jnp.concatenate(outs, axis=1)


def _fold_pool_into_out(w_grp_ref, scale_ref, w_out_ref, w_eff_scr):
    scale = scale_ref[...]
    for g in range(len(POOL_WINDOWS)):
        rows = slice(g * POOL_GC, (g + 1) * POOL_GC)
        a = w_grp_ref[g] * scale[:, rows]
        a_hi = a.astype(BF16)
        a_lo = (a - a_hi.astype(F32)).astype(BF16)
        b = w_out_ref[rows, :]
        w_eff_scr[rows, :] = (_dot(a_hi, b) + _dot(a_lo, b)).astype(BF16)
    w_eff_scr[D_POOL:, :] = w_out_ref[D_POOL:, :]


def _conv_rows(full_cx, conv_w):
    return (conv_w[0:1] * _shift_rows(full_cx, 2) + conv_w[1:2] * _shift_rows(full_cx, 1)
            + conv_w[2:3] * full_cx)


def _gains(gain_refs, layer):
    return [r[layer:layer + 1, :] for r in gain_refs]


def _in_proj(x, g_pre, w_in_ref):
    p = _dot(_rms(x, g_pre).astype(BF16), w_in_ref[...])
    u = p[:, :D_POOL]
    cx = p[:, D_POOL + 2 * D_CONV:] * p[:, D_POOL:D_POOL + D_CONV]
    gate_b = p[:, D_POOL + D_CONV:D_POOL + 2 * D_CONV]
    return u, cx, gate_b


def _ffn(x, g_pre, g_post, w_up_ref, w_down_ref):
    h = _rms(x, g_pre).astype(BF16)
    a = jnp.maximum(_dot(h, w_up_ref[...]), 0.0)
    a = (a * a).astype(BF16)
    f = _dot(a, w_down_ref[...])
    return x + _rms(f, g_post)


def _sub_spans(tm):
    assert tm % SUB_TILE_ROWS == 0
    return [(r0, SUB_TILE_ROWS) for r0 in range(0, tm, SUB_TILE_ROWS)]


def _span_groups(tm):
    spans = _sub_spans(tm)
    return [spans[k:k + SUB_GROUP] for k in range(0, len(spans), SUB_GROUP)]


def _ffn_staged(xs, g_pre, g_post, w_up_ref, w_down_ref):
    acts = []
    for x in xs:
        a = jnp.maximum(_dot(_rms(x, g_pre).astype(BF16), w_up_ref[...]), 0.0)
        acts.append((a * a).astype(BF16))
    return [x + _rms(_dot(a, w_down_ref[...]), g_post) for x, a in zip(xs, acts)]


def _stream_cast(srcs, slab, dsts, stage, sem):
    ahead = STAGE_SLOTS - 1

    def make_plan(src, dst, first):
        rows, cols = dst.shape
        assert rows % STAGE_ROWS == 0 and cols % STAGE_COLS == 0
        ncol = cols // STAGE_COLS
        n = (rows // STAGE_ROWS) * ncol
        assert n >= ahead

        def window(k):
            r0 = pl.multiple_of((k // ncol) * STAGE_ROWS, STAGE_ROWS)
            c0 = pl.multiple_of((k % ncol) * STAGE_COLS, STAGE_COLS)
            return pl.ds(r0, STAGE_ROWS), pl.ds(c0, STAGE_COLS)

        def fetch(k):
            rs, cs = window(k)
            slot = (first + k) % STAGE_SLOTS
            return pltpu.make_async_copy(src.at[slab, rs, cs], stage.at[slot], sem.at[slot])

        return n, window, fetch

    plans, first = [], 0
    for src, dst in zip(srcs, dsts):
        plans.append(make_plan(src, dst, first) + (first,))
        first += plans[-1][0]

    def start(fetch, k, g):
        if isinstance(g, int):
            fetch(k).start(priority=g % 2)
            return
        for prio in range(2):
            @pl.when(g % 2 == prio)
            def _(prio=prio):
                fetch(k).start(priority=prio)

    for k in range(ahead):
        start(plans[0][2], k, k)
    for j, ((n, window, fetch, first), dst) in enumerate(zip(plans, dsts)):
        next_fetch = plans[j + 1][2] if j + 1 < len(plans) else None

        def body(k, carry, n=n, window=window, fetch=fetch, first=first, dst=dst,
                 next_fetch=next_fetch):
            @pl.when(k + ahead < n)
            def _():
                start(fetch, k + ahead, first + k + ahead)

            if next_fetch is not None:
                @pl.when(k + ahead >= n)
                def _():
                    start(next_fetch, k + ahead - n, first + k + ahead)

            fetch(k).wait()
            rs, cs = window(k)
            dst[rs, cs] = stage[(first + k) % STAGE_SLOTS].astype(BF16)
            return carry

        lax.fori_loop(0, n, body, 0)


def _export_copies(bufs, outs, sem):
    return [pltpu.make_async_copy(buf, out, sem.at[j]) for j, (buf, out) in enumerate(zip(bufs, outs))]


def _layer0_prompt_kernel(x_ref, g0, g1, g2, g3, w_grp_ref, scale_ref, conv_w_ref,
                          *rest, layer, n_cast):
    w_hbm, cast_in, rest = rest[:4], rest[4:4 + n_cast], rest[4 + n_cast:]
    (y_ref, pool_out_ref, conv_out_ref), w_exp, rest = rest[:3], rest[3:7], rest[7:]
    cast_out, rest = rest[:n_cast], rest[n_cast:]
    halo_u, halo_cx, w_eff_scr, w_in_ref, w_out_ref, w_up_ref, w_down_ref, stage, sem_in, sem_out = rest
    w_bufs = (w_in_ref, w_out_ref, w_up_ref, w_down_ref)
    tm = x_ref.shape[0]
    t = pl.program_id(1)
    first = (pl.program_id(0) == 0) & (t == 0)
    last = (pl.program_id(0) == pl.num_programs(0) - 1) & (t == pl.num_programs(1) - 1)
    g_pre, g_post, gf_pre, gf_post = _gains((g0, g1, g2, g3), layer)

    for src, dst in zip(cast_in, cast_out):
        dst[...] = src[...].astype(BF16)

    @pl.when(first)
    def _():
        halo_u[...] = jnp.zeros_like(halo_u)
        halo_cx[...] = jnp.zeros_like(halo_cx)
        _stream_cast(w_hbm, layer // 2, w_bufs, stage, sem_in)
        for c in _export_copies(w_bufs, w_exp, sem_out):
            c.start()
        _fold_pool_into_out(w_grp_ref, scale_ref, w_out_ref, w_eff_scr)

    @pl.when(last)
    def _():
        for c in _export_copies(w_bufs, w_exp, sem_out):
            c.wait()

    fresh = t == 0
    hist_u = jnp.where(fresh, 0.0, halo_u[...])
    hist_cx = jnp.where(fresh, 0.0, halo_cx[...])

    for spans in _span_groups(tm):
        xs = [x_ref[pl.ds(r0, ts), :] for r0, ts in spans]
        proj = [_in_proj(x, g_pre, w_in_ref) for x in xs]

        yabs = []
        for (r0, ts), (u, cx, gate_b) in zip(spans, proj):
            full_u = jnp.concatenate([hist_u, u], axis=0)
            full_cx = jnp.concatenate([hist_cx, cx], axis=0)
            hist_u, hist_cx = u[ts - HALO:], cx[ts - HALO:]
            pos = (lax.broadcasted_iota(jnp.int32, (ts, 1), 0) + (t * tm + r0)).astype(F32)
            sums = [s[HALO:] for s in _window_sums(full_u)]
            d = _pool_diff(sums, u, pos)
            cz = _conv_rows(full_cx, conv_w_ref[...])[HALO:]
            yabs.append(jnp.concatenate([d, gate_b * cz], axis=1).astype(BF16))

        x1 = [x + _rms(_dot(yab, w_eff_scr[...]), g_post) for x, yab in zip(xs, yabs)]
        for (r0, ts), y in zip(spans, _ffn_staged(x1, gf_pre, gf_post, w_up_ref, w_down_ref)):
            y_ref[pl.ds(r0, ts), :] = y
    halo_u[...] = hist_u
    halo_cx[...] = hist_cx

    pool_out_ref[...] = halo_u[HALO - POOL_PREV:, :]
    conv_out_ref[...] = halo_cx[HALO - (CONV_W - 1):, :]


def _layer0_sample_kernel(x_ref, pool_ref, conv_ref, g0, g1, g2, g3, w_grp_ref, scale_ref,
                          conv_w_ref, w_in_ref, w_out_ref, w_up_ref, w_down_ref,
                          y_ref, pool_out_ref, conv_out_ref,
                          full_u_scr, full_cx_scr, w_eff_scr, *, layer, seq):
    tm = x_ref.shape[0]
    tb = tm // seq
    grp = HALO + seq
    g_pre, g_post, gf_pre, gf_post = _gains((g0, g1, g2, g3), layer)

    @pl.when(pl.program_id(0) == 0)
    def _():
        _fold_pool_into_out(w_grp_ref, scale_ref, w_out_ref, w_eff_scr)

    ts = tm // SUB_TILES_SAMPLE0
    tbs = tb // SUB_TILES_SAMPLE0
    subs = range(SUB_TILES_SAMPLE0)
    xs = [x_ref[pl.ds(sub * ts, ts), :] for sub in subs]
    proj = [_in_proj(x, g_pre, w_in_ref) for x in xs]

    def new_rows(a):
        return a.reshape(tbs, grp, a.shape[-1])[:, HALO:].reshape(ts, a.shape[-1])

    pos = (lax.broadcasted_iota(jnp.int32, (tbs, seq, 1), 1).reshape(ts, 1) + PAST_LEN).astype(F32)
    yabs = []
    for sub, (u, cx, gate_b) in enumerate(proj):
        sq = pl.ds(sub * tbs, tbs)
        full_u_scr[sq, :HALO - POOL_PREV, :] = jnp.zeros((tbs, HALO - POOL_PREV, D_POOL), F32)
        for r in range(POOL_PREV):
            full_u_scr[sq, HALO - POOL_PREV + r, :] = pool_ref[r, sq, :]
        full_u_scr[sq, HALO:, :] = u.reshape(tbs, seq, D_POOL)
        full_cx_scr[sq, :HALO - (CONV_W - 1), :] = jnp.zeros((tbs, HALO - (CONV_W - 1), D_CONV), F32)
        full_cx_scr[sq, HALO - (CONV_W - 1):HALO, :] = conv_ref[sq]
        full_cx_scr[sq, HALO:, :] = cx.reshape(tbs, seq, D_CONV)
        for r in range(POOL_PREV):
            pool_out_ref[r, sq, :] = full_u_scr[sq, grp - POOL_PREV + r, :]
        conv_out_ref[sq] = full_cx_scr[sq, grp - (CONV_W - 1):, :]
        full_u = full_u_scr[sq].reshape(tbs * grp, D_POOL)
        full_cx = full_cx_scr[sq].reshape(tbs * grp, D_CONV)
        sums = [new_rows(s) for s in _window_sums(full_u)]
        d = _pool_diff(sums, u, pos)
        cz = new_rows(_conv_rows(full_cx, conv_w_ref[...]))
        yabs.append(jnp.concatenate([d, gate_b * cz], axis=1).astype(BF16))

    x1 = [x + _rms(_dot(yab, w_eff_scr[...]), g_post) for x, yab in zip(xs, yabs)]
    for sub, y in enumerate(_ffn_staged(x1, gf_pre, gf_post, w_up_ref, w_down_ref)):
        y_ref[pl.ds(sub * ts, ts), :] = y


def _sgu_tables(w_sp_ref, b_sp_ref, wm_scr, bias_scr, seq):
    r = lax.broadcasted_iota(jnp.int32, (CHUNK, CHUNK), 0)
    c = lax.broadcasted_iota(jnp.int32, (CHUNK, CHUNK), 1)
    keep = r >= c
    b = b_sp_ref[...]
    if seq is not None:
        keep = keep & (r // seq == c // seq)
        pick = (r % seq == c).astype(BF16)
        pick_t = (c % seq == r).astype(BF16)
        b = jnp.where(lax.broadcasted_iota(jnp.int32, b.shape, 1) < seq, b, 0.0)
        k = seq
        while k < CHUNK:
            b = b + pltpu.roll(b, k, axis=1)
            k *= 2
    mask = keep.astype(F32)
    for hh in range(SGU_HEADS):
        w = w_sp_ref[hh]
        if seq is not None:
            w = _dot(_dot(pick, w.astype(BF16)).astype(BF16), pick_t)
        wm_scr[hh] = (w * mask).astype(BF16)
        rows = jnp.broadcast_to(b[hh:hh + 1, :], (CHUNK, SGU_HD))
        bias_scr[:, hh * SGU_HD:(hh + 1) * SGU_HD] = rows.T


def _layer1_kernel(x_ref, g0, g1, g2, g3, g_v_ref, w_sp_ref, b_sp_ref,
                   w_uv_ref, w_out_ref, w_up_ref, w_down_ref, y_ref, *rest, layer, seq, emit_v):
    if emit_v:
        v_ref, wm_scr, bias_scr = rest
    else:
        wm_scr, bias_scr = rest
    tm = x_ref.shape[0]
    g_pre, g_post, gf_pre, gf_post = _gains((g0, g1, g2, g3), layer)

    @pl.when(pl.program_id(0) == 0)
    def _():
        _sgu_tables(w_sp_ref, b_sp_ref, wm_scr, bias_scr, seq)

    bias = bias_scr[...]
    for spans in _span_groups(tm):
        xs = [x_ref[pl.ds(r0, ts), :] for r0, ts in spans]
        ps = [_dot(_rms(x, g_pre).astype(BF16), w_uv_ref[...]) for x in xs]

        gated = []
        for (r0, ts), p in zip(spans, ps):
            nc = ts // CHUNK
            u = p[:, :D_SGU]
            v = _rms(p[:, D_SGU:], g_v_ref[...])
            if emit_v:
                v_ref[pl.ds(r0, ts), :] = v
            vb = v.astype(BF16)
            cols = []
            for hh in range(SGU_HEADS):
                lanes = slice(hh * SGU_HD, (hh + 1) * SGU_HD)
                rhs = jnp.concatenate(
                    [vb[ci * CHUNK:(ci + 1) * CHUNK, lanes] for ci in range(nc)], axis=1)
                out = _dot(wm_scr[hh], rhs)
                cols.append(jnp.concatenate(
                    [out[:, ci * SGU_HD:(ci + 1) * SGU_HD] + bias[:, lanes] for ci in range(nc)],
                    axis=0))
            gated.append((u * jnp.concatenate(cols, axis=1)).astype(BF16))

        x1 = [x + _rms(_dot(a, w_out_ref[...]), g_post) for x, a in zip(xs, gated)]
        for (r0, ts), y in zip(spans, _ffn_staged(x1, gf_pre, gf_post, w_up_ref, w_down_ref)):
            y_ref[pl.ds(r0, ts), :] = y


def _resident(arr, layer=None):
    if layer is None:
        nd = arr.ndim
        return pl.BlockSpec(arr.shape, lambda *_: (0,) * nd, pipeline_mode=pl.Buffered(1))
    nd = arr.ndim - 1
    return pl.BlockSpec((None,) + arr.shape[1:], lambda *_: (layer,) + (0,) * nd,
                        pipeline_mode=pl.Buffered(1))


def _params(n_axes):
    return pltpu.CompilerParams(dimension_semantics=("arbitrary",) * n_axes,
                                vmem_limit_bytes=VMEM_LIMIT_BYTES)


def _split(weights):
    return [a for a, _ in weights], [_resident(a, l) for a, l in weights]


def _layer0_prompt(x, weights, big, layer, to_cast):
    b, s, d = x.shape
    tm = TILE_M_PROMPT
    nt = s // tm
    steps = b * nt
    arrs, specs = _split(weights)
    any_spec = pl.BlockSpec(memory_space=pl.ANY)
    big_shapes = [w.shape[1:] for w in big]
    tile = pl.BlockSpec((None, tm, d), lambda i, j: (i, j, 0))
    pool = pl.BlockSpec((None, None, POOL_PREV, D_POOL), lambda i, j: (0, i, 0, 0))
    conv = pl.BlockSpec((None, None, CONV_W - 1, D_CONV), lambda i, j: (0, i, 0, 0))
    cast_in, cast_out, cast_shapes = [], [], []
    for w, slab in to_cast:
        _, rows, cols = w.shape
        assert rows % steps == 0
        cast_in.append(pl.BlockSpec((None, rows // steps, cols),
                                    lambda i, j, slab=slab: (slab, i * nt + j, 0)))
        cast_out.append(pl.BlockSpec((rows // steps, cols), lambda i, j: (i * nt + j, 0)))
        cast_shapes.append(jax.ShapeDtypeStruct((rows, cols), BF16))
    return pl.pallas_call(
        functools.partial(_layer0_prompt_kernel, layer=layer, n_cast=len(to_cast)),
        grid=(b, nt),
        in_specs=[tile] + specs + [any_spec] * len(big) + cast_in,
        out_specs=[tile, pool, conv] + [any_spec] * len(big) + cast_out,
        out_shape=[jax.ShapeDtypeStruct(x.shape, F32),
                   jax.ShapeDtypeStruct((1, b, POOL_PREV, D_POOL), F32),
                   jax.ShapeDtypeStruct((1, b, CONV_W - 1, D_CONV), F32)]
        + [jax.ShapeDtypeStruct(sh, BF16) for sh in big_shapes] + cast_shapes,
        scratch_shapes=[pltpu.VMEM((HALO, D_POOL), F32), pltpu.VMEM((HALO, D_CONV), F32),
                        pltpu.VMEM((D_POOL + D_CONV, d), BF16)]
        + [pltpu.VMEM(sh, BF16) for sh in big_shapes]
        + [pltpu.VMEM((STAGE_SLOTS, STAGE_ROWS, STAGE_COLS), F32),
           pltpu.SemaphoreType.DMA((STAGE_SLOTS,)),
           pltpu.SemaphoreType.DMA((len(big),))],
        compiler_params=_params(2),
        name="layer0_prompt",
    )(x, *arrs, *big, *[w for w, _ in to_cast])


def _layer0_sample(x, state_pool, state_conv, weights, layer, seq):
    n, d = x.shape
    tm = TILE_M_SAMPLE
    tb = tm // seq
    arrs, specs = _split(weights)
    tile = pl.BlockSpec((tm, d), lambda i: (i, 0))
    pool_t = jnp.swapaxes(state_pool, 1, 2)
    pool = pl.BlockSpec((None, POOL_PREV, tb, D_POOL), lambda i: (0, 0, i, 0))
    conv = pl.BlockSpec((None, tb, CONV_W - 1, D_CONV), lambda i: (0, i, 0, 0))
    y, pool_new, conv_new = pl.pallas_call(
        functools.partial(_layer0_sample_kernel, layer=layer, seq=seq),
        grid=(n // tm,),
        in_specs=[tile, pool, conv] + specs,
        out_specs=[tile, pool, conv],
        out_shape=[jax.ShapeDtypeStruct(x.shape, F32),
                   jax.ShapeDtypeStruct(pool_t.shape, F32),
                   jax.ShapeDtypeStruct(state_conv.shape, F32)],
        scratch_shapes=[pltpu.VMEM((tb, HALO + seq, D_POOL), F32),
                        pltpu.VMEM((tb, HALO + seq, D_CONV), F32),
                        pltpu.VMEM((D_POOL + D_CONV, d), BF16)],
        compiler_params=_params(1),
        name="layer0_sample",
    )(x, pool_t, state_conv, *arrs)
    return y, jnp.swapaxes(pool_new, 1, 2), conv_new


def _layer1(x, weights, tm, layer, seq, emit_v, name):
    n, d = x.shape
    arrs, specs = _split(weights)
    tile = pl.BlockSpec((tm, d), lambda i: (i, 0))
    out_specs = [tile, tile] if emit_v else [tile]
    out_shape = [jax.ShapeDtypeStruct(x.shape, F32)] * len(out_specs)
    return pl.pallas_call(
        functools.partial(_layer1_kernel, layer=layer, seq=seq, emit_v=emit_v),
        grid=(n // tm,),
        in_specs=[tile] + specs,
        out_specs=out_specs,
        out_shape=out_shape,
        scratch_shapes=[pltpu.VMEM((SGU_HEADS, CHUNK, CHUNK), BF16),
                        pltpu.VMEM((CHUNK, D_SGU), F32)],
        compiler_params=_params(1),
        name=name,
    )(x, *arrs)


def kernel(x_prompt, x_sample, state_pool, state_conv, g_mix_pre, g_mix_post, g_ffn_pre,
           g_ffn_post, w_in_ab, w_pool_grp, pool_scale, conv_w, w_out_ab, w_uv, g_v, w_spatial,
           b_spatial, w_out_c, w_up, w_down):
    bp, sp, d = x_prompt.shape
    bs, ss, _ = x_sample.shape
    assert ss < CHUNK and CHUNK % ss == 0 and CONV_W - 1 <= ss and sp % CHUNK == 0

    gains = ((g_mix_pre, None), (g_mix_post, None), (g_ffn_pre, None), (g_ffn_post, None))
    w0_small = gains + ((w_pool_grp, 0), (pool_scale, None), (conv_w, 0))

    (y0, pool_p, conv_p, w_in_b, w_out_b, w_up0_b, w_down0_b,
     w_uv_b, w_out_c_b, w_up1_b, w_down1_b) = _layer0_prompt(
        x_prompt, w0_small, (w_in_ab, w_out_ab, w_up, w_down), 0,
        ((w_uv, 0), (w_out_c, 0), (w_up, 1), (w_down, 1)))
    w0 = w0_small + ((w_in_b, None), (w_out_b, None), (w_up0_b, None), (w_down0_b, None))
    w1 = gains + ((g_v, None), (w_spatial, 0), (b_spatial, 0),
                  (w_uv_b, None), (w_out_c_b, None), (w_up1_b, None), (w_down1_b, None))
    (y_prompt,) = _layer1(y0.reshape(bp * sp, d), w1, TILE_M_PROMPT, 1, None, False,
                          "layer1_prompt")

    ys0, pool_s, conv_s = _layer0_sample(x_sample.reshape(bs * ss, d), state_pool, state_conv,
                                         w0, 0, ss)
    ys1, v_s = _layer1(ys0, w1, TILE_M_SAMPLE, 1, ss, True, "layer1_sample")
    return (y_prompt.reshape(bp, sp, d), ys1.reshape(bs, ss, d), pool_p, pool_s, conv_p, conv_s,
            v_s.reshape(1, bs, ss, D_SGU))
```

```python
import functools

import jax
import jax.numpy as jnp
from jax import lax
from jax.experimental import pallas as pl
from jax.experimental.pallas import tpu as pltpu

D_MODEL = 1024
EPS = 1e-6
D_POOL = D_MODEL // 2
POOL_WINDOWS = (2, 4, 8, 16)
POOL_GC = D_POOL // len(POOL_WINDOWS)
POOL_PREV = max(POOL_WINDOWS) - 1
D_CONV = D_MODEL // 2
CONV_W = 3
D_SGU = D_MODEL
SGU_HEADS = 8
SGU_HD = D_SGU // SGU_HEADS
CHUNK = 128
PAST_LEN = 16384

HALO = 16
TILE_M_PROMPT = 512
TILE_M_SAMPLE = 512
SUB_TILE_ROWS = 256
SUB_GROUP = 2
TILE_M_PROMPT_L1 = 1024
SUB_TILE_ROWS_L1 = 512
SUB_TILES_SAMPLE0 = 1
STAGE_ROWS, STAGE_COLS = 512, 1024
STAGE_SLOTS = 3
VMEM_LIMIT_BYTES = 60 * 1024 * 1024

F32 = jnp.float32
BF16 = jnp.bfloat16


def _rms(x, g):
    return x * lax.rsqrt(jnp.mean(x * x, axis=-1, keepdims=True) + EPS) * g


def _dot(a, b):
    return jnp.dot(a, b, preferred_element_type=F32)


def _shift_rows(a, k):
    return pltpu.roll(a, k, axis=0)


def _window_sums(full_u):
    sums = []
    for g, w in enumerate(POOL_WINDOWS):
        s = full_u[:, g * POOL_GC:(g + 1) * POOL_GC]
        k = 1
        while k < w:
            s = s + _shift_rows(s, k)
            k *= 2
        sums.append(s)
    return sums


def _pool_diff(sums, u, pos):
    outs = []
    for g, w in enumerate(POOL_WINDOWS):
        inv_cnt = 1.0 / jnp.minimum(pos + 1.0, float(w))
        outs.append(sums[g] * inv_cnt - u[:, g * POOL_GC:(g + 1) * POOL_GC])
    return jnp.concatenate(outs, axis=1)


def _fold_pool_into_out(w_grp_ref, scale_ref, w_out_ref, w_eff_scr):
    scale = scale_ref[...]
    for g in range(len(POOL_WINDOWS)):
        rows = slice(g * POOL_GC, (g + 1) * POOL_GC)
        a = w_grp_ref[g] * scale[:, rows]
        a_hi = a.astype(BF16)
        a_lo = (a - a_hi.astype(F32)).astype(BF16)
        b = w_out_ref[rows, :]
        w_eff_scr[rows, :] = (_dot(a_hi, b) + _dot(a_lo, b)).astype(BF16)
    w_eff_scr[D_POOL:, :] = w_out_ref[D_POOL:, :]


def _conv_rows(full_cx, conv_w):
    return (conv_w[0:1] * _shift_rows(full_cx, 2) + conv_w[1:2] * _shift_rows(full_cx, 1)
            + conv_w[2:3] * full_cx)


def _gains(gain_refs, layer):
    return [r[layer:layer + 1, :] for r in gain_refs]


def _in_proj(x, g_pre, w_in_ref):
    p = _dot(_rms(x, g_pre).astype(BF16), w_in_ref[...])
    u = p[:, :D_POOL]
    cx = p[:, D_POOL + 2 * D_CONV:] * p[:, D_POOL:D_POOL + D_CONV]
    gate_b = p[:, D_POOL + D_CONV:D_POOL + 2 * D_CONV]
    return u, cx, gate_b


def _ffn(x, g_pre, g_post, w_up_ref, w_down_ref):
    h = _rms(x, g_pre).astype(BF16)
    a = jnp.maximum(_dot(h, w_up_ref[...]), 0.0)
    a = (a * a).astype(BF16)
    f = _dot(a, w_down_ref[...])
    return x + _rms(f, g_post)


def _span_groups(tm, sub_rows=SUB_TILE_ROWS):
    assert tm % sub_rows == 0
    spans = [(r0, sub_rows) for r0 in range(0, tm, sub_rows)]
    return [spans[k:k + SUB_GROUP] for k in range(0, len(spans), SUB_GROUP)]


def _ffn_staged(xs, g_pre, g_post, w_up_ref, w_down_ref):
    acts = []
    for x in xs:
        a = jnp.maximum(_dot(_rms(x, g_pre).astype(BF16), w_up_ref[...]), 0.0)
        acts.append((a * a).astype(BF16))
    return [x + _rms(_dot(a, w_down_ref[...]), g_post) for x, a in zip(xs, acts)]


def _stream_cast(srcs, slab, dsts, stage, sem):
    ahead = STAGE_SLOTS - 1

    def make_plan(src, dst, first):
        rows, cols = dst.shape
        assert rows % STAGE_ROWS == 0 and cols % STAGE_COLS == 0
        ncol = cols // STAGE_COLS
        n = (rows // STAGE_ROWS) * ncol
        assert n >= ahead

        def window(k):
            r0 = pl.multiple_of((k // ncol) * STAGE_ROWS, STAGE_ROWS)
            c0 = pl.multiple_of((k % ncol) * STAGE_COLS, STAGE_COLS)
            return pl.ds(r0, STAGE_ROWS), pl.ds(c0, STAGE_COLS)

        def fetch(k):
            rs, cs = window(k)
            slot = (first + k) % STAGE_SLOTS
            return pltpu.make_async_copy(src.at[slab, rs, cs], stage.at[slot], sem.at[slot])

        return n, window, fetch

    plans, first = [], 0
    for src, dst in zip(srcs, dsts):
        plans.append(make_plan(src, dst, first) + (first,))
        first += plans[-1][0]

    for k in range(ahead):
        plans[0][2](k).start()
    for j, ((n, window, fetch, first), dst) in enumerate(zip(plans, dsts)):
        next_fetch = plans[j + 1][2] if j + 1 < len(plans) else None

        def body(k, carry, n=n, window=window, fetch=fetch, first=first, dst=dst,
                 next_fetch=next_fetch):
            @pl.when(k + ahead < n)
            def _():
                fetch(k + ahead).start()

            if next_fetch is not None:
                @pl.when(k + ahead >= n)
                def _():
                    next_fetch(k + ahead - n).start()

            fetch(k).wait()
            rs, cs = window(k)
            dst[rs, cs] = stage[(first + k) % STAGE_SLOTS].astype(BF16)
            return carry

        lax.fori_loop(0, n, body, 0)


def _export_copies(bufs, outs, sem):
    return [pltpu.make_async_copy(buf, out, sem.at[j]) for j, (buf, out) in enumerate(zip(bufs, outs))]


def _layer0_prompt_kernel(x_ref, g0, g1, g2, g3, w_grp_ref, scale_ref, conv_w_ref,
                          *rest, layer, n_cast):
    w_hbm, cast_in, rest = rest[:4], rest[4:4 + n_cast], rest[4 + n_cast:]
    (y_ref, pool_out_ref, conv_out_ref), w_exp, rest = rest[:3], rest[3:7], rest[7:]
    cast_out, rest = rest[:n_cast], rest[n_cast:]
    halo_u, halo_cx, w_eff_scr, w_in_ref, w_out_ref, w_up_ref, w_down_ref, stage, sem_in, sem_out = rest
    w_bufs = (w_in_ref, w_out_ref, w_up_ref, w_down_ref)
    tm = x_ref.shape[0]
    t = pl.program_id(1)
    first = (pl.program_id(0) == 0) & (t == 0)
    last = (pl.program_id(0) == pl.num_programs(0) - 1) & (t == pl.num_programs(1) - 1)
    g_pre, g_post, gf_pre, gf_post = _gains((g0, g1, g2, g3), layer)

    for src, dst in zip(cast_in, cast_out):
        dst[...] = src[...].astype(BF16)

    @pl.when(first)
    def _():
        _stream_cast(w_hbm, layer // 2, w_bufs, stage, sem_in)
        for c in _export_copies(w_bufs, w_exp, sem_out):
            c.start()
        _fold_pool_into_out(w_grp_ref, scale_ref, w_out_ref, w_eff_scr)

    @pl.when(last)
    def _():
        for c in _export_copies(w_bufs, w_exp, sem_out):
            c.wait()

    @pl.when(t == 0)
    def _():
        halo_u[...] = jnp.zeros_like(halo_u)
        halo_cx[...] = jnp.zeros_like(halo_cx)

    hist_u, hist_cx = halo_u[...], halo_cx[...]
    for spans in _span_groups(tm):
        xs = [x_ref[pl.ds(r0, ts), :] for r0, ts in spans]
        proj = [_in_proj(x, g_pre, w_in_ref) for x in xs]

        yabs = []
        for (r0, ts), (u, cx, gate_b) in zip(spans, proj):
            full_u = jnp.concatenate([hist_u, u], axis=0)
            full_cx = jnp.concatenate([hist_cx, cx], axis=0)
            hist_u, hist_cx = u[ts - HALO:], cx[ts - HALO:]
            pos = (lax.broadcasted_iota(jnp.int32, (ts, 1), 0) + (t * tm + r0)).astype(F32)
            sums = [s[HALO:] for s in _window_sums(full_u)]
            d = _pool_diff(sums, u, pos)
            cz = _conv_rows(full_cx, conv_w_ref[...])[HALO:]
            yabs.append(jnp.concatenate([d, gate_b * cz], axis=1).astype(BF16))

        x1 = [x + _rms(_dot(yab, w_eff_scr[...]), g_post) for x, yab in zip(xs, yabs)]
        for (r0, ts), y in zip(spans, _ffn_staged(x1, gf_pre, gf_post, w_up_ref, w_down_ref)):
            y_ref[pl.ds(r0, ts), :] = y
    halo_u[...] = hist_u
    halo_cx[...] = hist_cx

    pool_out_ref[...] = halo_u[HALO - POOL_PREV:, :]
    conv_out_ref[...] = halo_cx[HALO - (CONV_W - 1):, :]


def _layer0_sample_kernel(x_ref, pool_ref, conv_ref, g0, g1, g2, g3, w_grp_ref, scale_ref,
                          conv_w_ref, w_in_ref, w_out_ref, w_up_ref, w_down_ref,
                          y_ref, pool_out_ref, conv_out_ref,
                          full_u_scr, full_cx_scr, w_eff_scr, *, layer, seq):
    tm = x_ref.shape[0]
    tb = tm // seq
    grp = HALO + seq
    g_pre, g_post, gf_pre, gf_post = _gains((g0, g1, g2, g3), layer)

    @pl.when(pl.program_id(0) == 0)
    def _():
        _fold_pool_into_out(w_grp_ref, scale_ref, w_out_ref, w_eff_scr)

    ts = tm // SUB_TILES_SAMPLE0
    tbs = tb // SUB_TILES_SAMPLE0
    subs = range(SUB_TILES_SAMPLE0)
    xs = [x_ref[pl.ds(sub * ts, ts), :] for sub in subs]
    proj = [_in_proj(x, g_pre, w_in_ref) for x in xs]

    def new_rows(a):
        return a.reshape(tbs, grp, a.shape[-1])[:, HALO:].reshape(ts, a.shape[-1])

    pos = (lax.broadcasted_iota(jnp.int32, (tbs, seq, 1), 1).reshape(ts, 1) + PAST_LEN).astype(F32)
    yabs = []
    for sub, (u, cx, gate_b) in enumerate(proj):
        sq = pl.ds(sub * tbs, tbs)
        full_u_scr[sq, :HALO - POOL_PREV, :] = jnp.zeros((tbs, HALO - POOL_PREV, D_POOL), F32)
        for r in range(POOL_PREV):
            full_u_scr[sq, HALO - POOL_PREV + r, :] = pool_ref[r, sq, :]
        full_u_scr[sq, HALO:, :] = u.reshape(tbs, seq, D_POOL)
        full_cx_scr[sq, :HALO - (CONV_W - 1), :] = jnp.zeros((tbs, HALO - (CONV_W - 1), D_CONV), F32)
        full_cx_scr[sq, HALO - (CONV_W - 1):HALO, :] = conv_ref[sq]
        full_cx_scr[sq, HALO:, :] = cx.reshape(tbs, seq, D_CONV)
        for r in range(POOL_PREV):
            pool_out_ref[r, sq, :] = full_u_scr[sq, grp - POOL_PREV + r, :]
        conv_out_ref[sq] = full_cx_scr[sq, grp - (CONV_W - 1):, :]
        full_u = full_u_scr[sq].reshape(tbs * grp, D_POOL)
        full_cx = full_cx_scr[sq].reshape(tbs * grp, D_CONV)
        sums = [new_rows(s) for s in _window_sums(full_u)]
        d = _pool_diff(sums, u, pos)
        cz = new_rows(_conv_rows(full_cx, conv_w_ref[...]))
        yabs.append(jnp.concatenate([d, gate_b * cz], axis=1).astype(BF16))

    x1 = [x + _rms(_dot(yab, w_eff_scr[...]), g_post) for x, yab in zip(xs, yabs)]
    for sub, y in enumerate(_ffn_staged(x1, gf_pre, gf_post, w_up_ref, w_down_ref)):
        y_ref[pl.ds(sub * ts, ts), :] = y


def _sgu_tables(w_sp_ref, b_sp_ref, wm_scr, bias_scr, seq):
    r = lax.broadcasted_iota(jnp.int32, (CHUNK, CHUNK), 0)
    c = lax.broadcasted_iota(jnp.int32, (CHUNK, CHUNK), 1)
    keep = r >= c
    b = b_sp_ref[...]
    if seq is not None:
        keep = keep & (r // seq == c // seq)
        pick = (r % seq == c).astype(BF16)
        pick_t = (c % seq == r).astype(BF16)
        b = jnp.where(lax.broadcasted_iota(jnp.int32, b.shape, 1) < seq, b, 0.0)
        k = seq
        while k < CHUNK:
            b = b + pltpu.roll(b, k, axis=1)
            k *= 2
    mask = keep.astype(F32)
    for hh in range(SGU_HEADS):
        w = w_sp_ref[hh]
        if seq is not None:
            w = _dot(_dot(pick, w.astype(BF16)).astype(BF16), pick_t)
        wm_scr[hh] = (w * mask).astype(BF16)
        rows = jnp.broadcast_to(b[hh:hh + 1, :], (CHUNK, SGU_HD))
        bias_scr[:, hh * SGU_HD:(hh + 1) * SGU_HD] = rows.T


def _layer1_kernel(x_ref, g0, g1, g2, g3, g_v_ref, w_sp_ref, b_sp_ref,
                   w_uv_ref, w_out_ref, w_up_ref, w_down_ref, y_ref, *rest, layer, seq, emit_v,
                   sub_rows):
    if emit_v:
        v_ref, wm_scr, bias_scr = rest
    else:
        wm_scr, bias_scr = rest
    tm = x_ref.shape[0]
    g_pre, g_post, gf_pre, gf_post = _gains((g0, g1, g2, g3), layer)

    @pl.when(pl.program_id(0) == 0)
    def _():
        _sgu_tables(w_sp_ref, b_sp_ref, wm_scr, bias_scr, seq)

    bias = bias_scr[...]
    for spans in _span_groups(tm, sub_rows):
        xs = [x_ref[pl.ds(r0, ts), :] for r0, ts in spans]
        ps = [_dot(_rms(x, g_pre).astype(BF16), w_uv_ref[...]) for x in xs]

        gated = []
        for (r0, ts), p in zip(spans, ps):
            nc = ts // CHUNK
            u = p[:, :D_SGU]
            v = _rms(p[:, D_SGU:], g_v_ref[...])
            if emit_v:
                v_ref[pl.ds(r0, ts), :] = v
            vb = v.astype(BF16)
            cols = []
            for hh in range(SGU_HEADS):
                lanes = slice(hh * SGU_HD, (hh + 1) * SGU_HD)
                rhs = jnp.concatenate(
                    [vb[ci * CHUNK:(ci + 1) * CHUNK, lanes] for ci in range(nc)], axis=1)
                out = _dot(wm_scr[hh], rhs)
                cols.append(jnp.concatenate(
                    [out[:, ci * SGU_HD:(ci + 1) * SGU_HD] + bias[:, lanes] for ci in range(nc)],
                    axis=0))
            gated.append((u * jnp.concatenate(cols, axis=1)).astype(BF16))

        x1 = [x + _rms(_dot(a, w_out_ref[...]), g_post) for x, a in zip(xs, gated)]
        for (r0, ts), y in zip(spans, _ffn_staged(x1, gf_pre, gf_post, w_up_ref, w_down_ref)):
            y_ref[pl.ds(r0, ts), :] = y


def _resident(arr, layer=None):
    if layer is None:
        nd = arr.ndim
        return pl.BlockSpec(arr.shape, lambda *_: (0,) * nd, pipeline_mode=pl.Buffered(1))
    nd = arr.ndim - 1
    return pl.BlockSpec((None,) + arr.shape[1:], lambda *_: (layer,) + (0,) * nd,
                        pipeline_mode=pl.Buffered(1))


def _params(n_axes):
    return pltpu.CompilerParams(dimension_semantics=("arbitrary",) * n_axes,
                                vmem_limit_bytes=VMEM_LIMIT_BYTES)


def _split(weights):
    return [a for a, _ in weights], [_resident(a, l) for a, l in weights]


def _layer0_prompt(x, weights, big, layer, to_cast):
    b, s, d = x.shape
    tm = TILE_M_PROMPT
    nt = s // tm
    steps = b * nt
    arrs, specs = _split(weights)
    any_spec = pl.BlockSpec(memory_space=pl.ANY)
    big_shapes = [w.shape[1:] for w in big]
    tile = pl.BlockSpec((None, tm, d), lambda i, j: (i, j, 0))
    pool = pl.BlockSpec((None, None, POOL_PREV, D_POOL), lambda i, j: (0, i, 0, 0))
    conv = pl.BlockSpec((None, None, CONV_W - 1, D_CONV), lambda i, j: (0, i, 0, 0))
    cast_in, cast_out, cast_shapes = [], [], []
    for w, slab in to_cast:
        _, rows, cols = w.shape
        assert rows % steps == 0
        cast_in.append(pl.BlockSpec((None, rows // steps, cols),
                                    lambda i, j, slab=slab: (slab, i * nt + j, 0)))
        cast_out.append(pl.BlockSpec((rows // steps, cols), lambda i, j: (i * nt + j, 0)))
        cast_shapes.append(jax.ShapeDtypeStruct((rows, cols), BF16))
    return pl.pallas_call(
        functools.partial(_layer0_prompt_kernel, layer=layer, n_cast=len(to_cast)),
        grid=(b, nt),
        in_specs=[tile] + specs + [any_spec] * len(big) + cast_in,
        out_specs=[tile, pool, conv] + [any_spec] * len(big) + cast_out,
        out_shape=[jax.ShapeDtypeStruct(x.shape, F32),
                   jax.ShapeDtypeStruct((1, b, POOL_PREV, D_POOL), F32),
                   jax.ShapeDtypeStruct((1, b, CONV_W - 1, D_CONV), F32)]
        + [jax.ShapeDtypeStruct(sh, BF16) for sh in big_shapes] + cast_shapes,
        scratch_shapes=[pltpu.VMEM((HALO, D_POOL), F32), pltpu.VMEM((HALO, D_CONV), F32),
                        pltpu.VMEM((D_POOL + D_CONV, d), BF16)]
        + [pltpu.VMEM(sh, BF16) for sh in big_shapes]
        + [pltpu.VMEM((STAGE_SLOTS, STAGE_ROWS, STAGE_COLS), F32),
           pltpu.SemaphoreType.DMA((STAGE_SLOTS,)),
           pltpu.SemaphoreType.DMA((len(big),))],
        compiler_params=_params(2),
        name="layer0_prompt",
    )(x, *arrs, *big, *[w for w, _ in to_cast])


def _layer0_sample(x, state_pool, state_conv, weights, layer, seq):
    n, d = x.shape
    tm = TILE_M_SAMPLE
    tb = tm // seq
    arrs, specs = _split(weights)
    tile = pl.BlockSpec((tm, d), lambda i: (i, 0))
    pool_t = jnp.swapaxes(state_pool, 1, 2)
    pool = pl.BlockSpec((None, POOL_PREV, tb, D_POOL), lambda i: (0, 0, i, 0))
    conv = pl.BlockSpec((None, tb, CONV_W - 1, D_CONV), lambda i: (0, i, 0, 0))
    y, pool_new, conv_new = pl.pallas_call(
        functools.partial(_layer0_sample_kernel, layer=layer, seq=seq),
        grid=(n // tm,),
        in_specs=[tile, pool, conv] + specs,
        out_specs=[tile, pool, conv],
        out_shape=[jax.ShapeDtypeStruct(x.shape, F32),
                   jax.ShapeDtypeStruct(pool_t.shape, F32),
                   jax.ShapeDtypeStruct(state_conv.shape, F32)],
        scratch_shapes=[pltpu.VMEM((tb, HALO + seq, D_POOL), F32),
                        pltpu.VMEM((tb, HALO + seq, D_CONV), F32),
                        pltpu.VMEM((D_POOL + D_CONV, d), BF16)],
        compiler_params=_params(1),
        name="layer0_sample",
    )(x, pool_t, state_conv, *arrs)
    return y, jnp.swapaxes(pool_new, 1, 2), conv_new


def _layer1(x, weights, tm, sub_rows, layer, seq, emit_v, name):
    n, d = x.shape
    arrs, specs = _split(weights)
    tile = pl.BlockSpec((tm, d), lambda i: (i, 0))
    out_specs = [tile, tile] if emit_v else [tile]
    out_shape = [jax.ShapeDtypeStruct(x.shape, F32)] * len(out_specs)
    return pl.pallas_call(
        functools.partial(_layer1_kernel, layer=layer, seq=seq, emit_v=emit_v,
                          sub_rows=sub_rows),
        grid=(n // tm,),
        in_specs=[tile] + specs,
        out_specs=out_specs,
        out_shape=out_shape,
        scratch_shapes=[pltpu.VMEM((SGU_HEADS, CHUNK, CHUNK), BF16),
                        pltpu.VMEM((CHUNK, D_SGU), F32)],
        compiler_params=_params(1),
        name=name,
    )(x, *arrs)


def kernel(x_prompt, x_sample, state_pool, state_conv, g_mix_pre, g_mix_post, g_ffn_pre,
           g_ffn_post, w_in_ab, w_pool_grp, pool_scale, conv_w, w_out_ab, w_uv, g_v, w_spatial,
           b_spatial, w_out_c, w_up, w_down):
    bp, sp, d = x_prompt.shape
    bs, ss, _ = x_sample.shape
    assert ss < CHUNK and CHUNK % ss == 0 and CONV_W - 1 <= ss and sp % CHUNK == 0

    gains = ((g_mix_pre, None), (g_mix_post, None), (g_ffn_pre, None), (g_ffn_post, None))
    w0_small = gains + ((w_pool_grp, 0), (pool_scale, None), (conv_w, 0))

    (y0, pool_p, conv_p, w_in_b, w_out_b, w_up0_b, w_down0_b,
     w_uv_b, w_out_c_b, w_up1_b, w_down1_b) = _layer0_prompt(
        x_prompt, w0_small, (w_in_ab, w_out_ab, w_up, w_down), 0,
        ((w_uv, 0), (w_out_c, 0), (w_up, 1), (w_down, 1)))
    w0 = w0_small + ((w_in_b, None), (w_out_b, None), (w_up0_b, None), (w_down0_b, None))
    w1 = gains + ((g_v, None), (w_spatial, 0), (b_spatial, 0),
                  (w_uv_b, None), (w_out_c_b, None), (w_up1_b, None), (w_down1_b, None))
    (y_prompt,) = _layer1(y0.reshape(bp * sp, d), w1, TILE_M_PROMPT_L1, SUB_TILE_ROWS_L1, 1, None,
                          False, "layer1_prompt")

    ys0, pool_s, conv_s = _layer0_sample(x_sample.reshape(bs * ss, d), state_pool, state_conv,
                                         w0, 0, ss)
    ys1, v_s = _layer1(ys0, w1, TILE_M_SAMPLE, SUB_TILE_ROWS, 1, ss, True, "layer1_sample")
    return (y_prompt.reshape(bp, sp, d), ys1.reshape(bs, ss, d), pool_p, pool_s, conv_p, conv_s,
            v_s.reshape(1, bs, ss, D_SGU))
```

```python
import functools

import jax
import jax.numpy as jnp
from jax import lax
from jax.experimental import pallas as pl
from jax.experimental.pallas import tpu as pltpu

D_MODEL = 1024
EPS = 1e-6
D_POOL = D_MODEL // 2
POOL_WINDOWS = (2, 4, 8, 16)
POOL_GC = D_POOL // len(POOL_WINDOWS)
POOL_PREV = max(POOL_WINDOWS) - 1
D_CONV = D_MODEL // 2
CONV_W = 3
D_SGU = D_MODEL
SGU_HEADS = 8
SGU_HD = D_SGU // SGU_HEADS
CHUNK = 128
PAST_LEN = 16384

HALO = 16
TILE_M_PROMPT = 512
TILE_M_SAMPLE = 512
SUB_TILE_ROWS = 256
SUB_GROUP = 2
TILE_M_PROMPT_L1 = 1024
SUB_TILE_ROWS_L1 = 512
STAGE_ROWS, STAGE_COLS = 512, 1024
STAGE_SLOTS = 3
VMEM_LIMIT_BYTES = 60 * 1024 * 1024

F32 = jnp.float32
BF16 = jnp.bfloat16


def _rms(x, g):
    return x * lax.rsqrt(jnp.mean(x * x, axis=-1, keepdims=True) + EPS) * g


def _dot(a, b):
    return jnp.dot(a, b, preferred_element_type=F32)


def _shift_rows(a, k):
    return pltpu.roll(a, k, axis=0)


def _window_sums(full_u):
    sums = []
    for g, w in enumerate(POOL_WINDOWS):
        s = full_u[:, g * POOL_GC:(g + 1) * POOL_GC]
        k = 1
        while k < w:
            s = s + _shift_rows(s, k)
            k *= 2
        sums.append(s)
    return sums


def _pool_diff(sums, u, pos):
    outs = []
    for g, w in enumerate(POOL_WINDOWS):
        inv_cnt = 1.0 / jnp.minimum(pos + 1.0, float(w))
        outs.append(sums[g] * inv_cnt - u[:, g * POOL_GC:(g + 1) * POOL_GC])
    return jnp.concatenate(outs, axis=1)


def _fold_pool_into_out(w_grp_ref, scale_ref, w_out_ref, w_eff_scr):
    scale = scale_ref[...]
    for g in range(len(POOL_WINDOWS)):
        rows = slice(g * POOL_GC, (g + 1) * POOL_GC)
        a = w_grp_ref[g] * scale[:, rows]
        a_hi = a.astype(BF16)
        a_lo = (a - a_hi.astype(F32)).astype(BF16)
        b = w_out_ref[rows, :]
        w_eff_scr[rows, :] = (_dot(a_hi, b) + _dot(a_lo, b)).astype(BF16)
    w_eff_scr[D_POOL:, :] = w_out_ref[D_POOL:, :]


def _conv_rows(full_cx, conv_w):
    return (conv_w[0:1] * _shift_rows(full_cx, 2) + conv_w[1:2] * _shift_rows(full_cx, 1)
            + conv_w[2:3] * full_cx)


def _gains(gain_refs, layer):
    return [r[layer:layer + 1, :] for r in gain_refs]


def _in_proj(x, g_pre, w_in_ref, before_dot=None):
    h = _rms(x, g_pre).astype(BF16)
    if before_dot is not None:
        before_dot()
    p = _dot(h, w_in_ref[...])
    u = p[:, :D_POOL]
    cx = p[:, D_POOL + 2 * D_CONV:] * p[:, D_POOL:D_POOL + D_CONV]
    gate_b = p[:, D_POOL + D_CONV:D_POOL + 2 * D_CONV]
    return u, cx, gate_b


def _ffn_lazy(x, g_pre, g_post, w_up_ref, w_down_ref, ready, k_up):
    h = _rms(x, g_pre).astype(BF16)
    ready(k_up)
    a = jnp.maximum(_dot(h, w_up_ref[...]), 0.0)
    a = (a * a).astype(BF16)
    ready(k_up + 1)
    f = _dot(a, w_down_ref[...])
    return x + _rms(f, g_post)


def _span_groups(tm, sub_rows=SUB_TILE_ROWS):
    assert tm % sub_rows == 0
    spans = [(r0, sub_rows) for r0 in range(0, tm, sub_rows)]
    return [spans[k:k + SUB_GROUP] for k in range(0, len(spans), SUB_GROUP)]


def _ffn_staged(xs, g_pre, g_post, w_up_ref, w_down_ref, ready=lambda k: None, k_up=0):
    hs = [_rms(x, g_pre).astype(BF16) for x in xs]
    ready(k_up)
    acts = []
    for h in hs:
        a = jnp.maximum(_dot(h, w_up_ref[...]), 0.0)
        acts.append((a * a).astype(BF16))
    ready(k_up + 1)
    return [x + _rms(_dot(a, w_down_ref[...]), g_post) for x, a in zip(xs, acts)]


def _stream_cast(srcs, slab, dsts, stage, sem):
    ahead = STAGE_SLOTS - 1

    def make_plan(src, dst, first):
        rows, cols = dst.shape
        assert rows % STAGE_ROWS == 0 and cols % STAGE_COLS == 0
        ncol = cols // STAGE_COLS
        n = (rows // STAGE_ROWS) * ncol
        assert n >= ahead

        def window(k):
            r0 = pl.multiple_of((k // ncol) * STAGE_ROWS, STAGE_ROWS)
            c0 = pl.multiple_of((k % ncol) * STAGE_COLS, STAGE_COLS)
            return pl.ds(r0, STAGE_ROWS), pl.ds(c0, STAGE_COLS)

        def fetch(k):
            rs, cs = window(k)
            slot = (first + k) % STAGE_SLOTS
            return pltpu.make_async_copy(src.at[slab, rs, cs], stage.at[slot], sem.at[slot])

        return n, window, fetch

    plans, first = [], 0
    for src, dst in zip(srcs, dsts):
        plans.append(make_plan(src, dst, first) + (first,))
        first += plans[-1][0]

    for k in range(ahead):
        plans[0][2](k).start()
    for j, ((n, window, fetch, first), dst) in enumerate(zip(plans, dsts)):
        next_fetch = plans[j + 1][2] if j + 1 < len(plans) else None

        def body(k, carry, n=n, window=window, fetch=fetch, first=first, dst=dst,
                 next_fetch=next_fetch):
            @pl.when(k + ahead < n)
            def _():
                fetch(k + ahead).start()

            if next_fetch is not None:
                @pl.when(k + ahead >= n)
                def _():
                    next_fetch(k + ahead - n).start()

            fetch(k).wait()
            rs, cs = window(k)
            dst[rs, cs] = stage[(first + k) % STAGE_SLOTS].astype(BF16)
            return carry

        lax.fori_loop(0, n, body, 0)


def _export_copies(bufs, outs, sem):
    return [pltpu.make_async_copy(buf, out, sem.at[j]) for j, (buf, out) in enumerate(zip(bufs, outs))]


def _layer0_prompt_kernel(x_ref, g0, g1, g2, g3, w_grp_ref, scale_ref, conv_w_ref,
                          *rest, layer, n_cast):
    w_hbm, cast_in, rest = rest[:4], rest[4:4 + n_cast], rest[4 + n_cast:]
    (y_ref, pool_out_ref, conv_out_ref), w_exp, rest = rest[:3], rest[3:7], rest[7:]
    cast_out, rest = rest[:n_cast], rest[n_cast:]
    halo_u, halo_cx, w_eff_scr, w_in_ref, w_out_ref, w_up_ref, w_down_ref, stage, sem_in, sem_out = rest
    w_bufs = (w_in_ref, w_out_ref, w_up_ref, w_down_ref)
    exported = (w_in_ref, w_eff_scr, w_up_ref, w_down_ref)
    tm = x_ref.shape[0]
    t = pl.program_id(1)
    first = (pl.program_id(0) == 0) & (t == 0)
    last = (pl.program_id(0) == pl.num_programs(0) - 1) & (t == pl.num_programs(1) - 1)
    g_pre, g_post, gf_pre, gf_post = _gains((g0, g1, g2, g3), layer)

    for src, dst in zip(cast_in, cast_out):
        dst[...] = src[...].astype(BF16)

    @pl.when(first)
    def _():
        _stream_cast(w_hbm, layer // 2, w_bufs, stage, sem_in)
        _fold_pool_into_out(w_grp_ref, scale_ref, w_out_ref, w_eff_scr)
        for c in _export_copies(exported, w_exp, sem_out):
            c.start()

    @pl.when(last)
    def _():
        for c in _export_copies(exported, w_exp, sem_out):
            c.wait()

    @pl.when(t == 0)
    def _():
        halo_u[...] = jnp.zeros_like(halo_u)
        halo_cx[...] = jnp.zeros_like(halo_cx)

    hist_u, hist_cx = halo_u[...], halo_cx[...]
    for spans in _span_groups(tm):
        xs = [x_ref[pl.ds(r0, ts), :] for r0, ts in spans]
        proj = [_in_proj(x, g_pre, w_in_ref) for x in xs]

        yabs = []
        for (r0, ts), (u, cx, gate_b) in zip(spans, proj):
            full_u = jnp.concatenate([hist_u, u], axis=0)
            full_cx = jnp.concatenate([hist_cx, cx], axis=0)
            hist_u, hist_cx = u[ts - HALO:], cx[ts - HALO:]
            pos = (lax.broadcasted_iota(jnp.int32, (ts, 1), 0) + (t * tm + r0)).astype(F32)
            sums = [s[HALO:] for s in _window_sums(full_u)]
            d = _pool_diff(sums, u, pos)
            cz = _conv_rows(full_cx, conv_w_ref[...])[HALO:]
            yabs.append(jnp.concatenate([d, gate_b * cz], axis=1).astype(BF16))

        x1 = [x + _rms(_dot(yab, w_eff_scr[...]), g_post) for x, yab in zip(xs, yabs)]
        for (r0, ts), y in zip(spans, _ffn_staged(x1, gf_pre, gf_post, w_up_ref, w_down_ref)):
            y_ref[pl.ds(r0, ts), :] = y
    halo_u[...] = hist_u
    halo_cx[...] = hist_cx

    pool_out_ref[...] = halo_u[HALO - POOL_PREV:, :]
    conv_out_ref[...] = halo_cx[HALO - (CONV_W - 1):, :]


def _lazy_weights(w_hbm, w_bufs, sem, tile_fn):
    copies = [pltpu.make_async_copy(h, b, sem.at[k]) for k, (h, b) in enumerate(zip(w_hbm, w_bufs))]
    first = pl.program_id(0) == 0

    @pl.when(first)
    def _():
        for c in copies:
            c.start()
        waited = set()

        def ready(k):
            if k not in waited:
                waited.add(k)
                copies[k].wait()

        tile_fn(ready)
        assert len(waited) == len(copies)

    @pl.when(jnp.logical_not(first))
    def _():
        tile_fn(lambda k: None)


def _layer0_sample_kernel(x_ref, pool_ref, conv_ref, g0, g1, g2, g3, conv_w_ref,
                          w_in_hbm, w_eff_hbm, w_up_hbm, w_down_hbm,
                          y_ref, pool_out_ref, conv_out_ref,
                          full_u_scr, full_cx_scr, w_in_ref, w_eff_ref, w_up_ref, w_down_ref, sem,
                          *, layer, seq):
    tm = x_ref.shape[0]
    tb = tm // seq
    grp = HALO + seq
    g_pre, g_post, gf_pre, gf_post = _gains((g0, g1, g2, g3), layer)

    def new_rows(a):
        return a.reshape(tb, grp, a.shape[-1])[:, HALO:].reshape(tm, a.shape[-1])

    def tile(ready):
        x = x_ref[...]
        u, cx, gate_b = _in_proj(x, g_pre, w_in_ref, lambda: ready(0))

        full_u_scr[:, :HALO - POOL_PREV, :] = jnp.zeros((tb, HALO - POOL_PREV, D_POOL), F32)
        for r in range(POOL_PREV):
            full_u_scr[:, HALO - POOL_PREV + r, :] = pool_ref[r]
        full_u_scr[:, HALO:, :] = u.reshape(tb, seq, D_POOL)
        full_cx_scr[:, :HALO - (CONV_W - 1), :] = jnp.zeros((tb, HALO - (CONV_W - 1), D_CONV), F32)
        full_cx_scr[:, HALO - (CONV_W - 1):HALO, :] = conv_ref[...]
        full_cx_scr[:, HALO:, :] = cx.reshape(tb, seq, D_CONV)
        for r in range(POOL_PREV):
            pool_out_ref[r] = full_u_scr[:, grp - POOL_PREV + r, :]
        conv_out_ref[...] = full_cx_scr[:, grp - (CONV_W - 1):, :]
        full_u = full_u_scr[...].reshape(tb * grp, D_POOL)
        full_cx = full_cx_scr[...].reshape(tb * grp, D_CONV)
        pos = (lax.broadcasted_iota(jnp.int32, (tb, seq, 1), 1).reshape(tm, 1)
               + PAST_LEN).astype(F32)
        d = _pool_diff([new_rows(s) for s in _window_sums(full_u)], u, pos)
        cz = new_rows(_conv_rows(full_cx, conv_w_ref[...]))
        yab = jnp.concatenate([d, gate_b * cz], axis=1).astype(BF16)
        ready(1)
        x = x + _rms(_dot(yab, w_eff_ref[...]), g_post)
        y_ref[...] = _ffn_lazy(x, gf_pre, gf_post, w_up_ref, w_down_ref, ready, 2)

    _lazy_weights((w_in_hbm, w_eff_hbm, w_up_hbm, w_down_hbm),
                  (w_in_ref, w_eff_ref, w_up_ref, w_down_ref), sem, tile)


def _sgu_tables(w_sp_ref, b_sp_ref, wm_scr, bias_scr, seq):
    r = lax.broadcasted_iota(jnp.int32, (CHUNK, CHUNK), 0)
    c = lax.broadcasted_iota(jnp.int32, (CHUNK, CHUNK), 1)
    keep = r >= c
    b = b_sp_ref[...]
    if seq is not None:
        keep = keep & (r // seq == c // seq)
        pick = (r % seq == c).astype(BF16)
        pick_t = (c % seq == r).astype(BF16)
        b = jnp.where(lax.broadcasted_iota(jnp.int32, b.shape, 1) < seq, b, 0.0)
        k = seq
        while k < CHUNK:
            b = b + pltpu.roll(b, k, axis=1)
            k *= 2
    mask = keep.astype(F32)
    for hh in range(SGU_HEADS):
        w = w_sp_ref[hh]
        if seq is not None:
            w = _dot(_dot(pick, w.astype(BF16)).astype(BF16), pick_t)
        wm_scr[hh] = (w * mask).astype(BF16)
        rows = jnp.broadcast_to(b[hh:hh + 1, :], (CHUNK, SGU_HD))
        bias_scr[:, hh * SGU_HD:(hh + 1) * SGU_HD] = rows.T


def _layer1_kernel(x_ref, g0, g1, g2, g3, g_v_ref, w_sp_ref, b_sp_ref,
                   w_uv_ref, w_out_ref, w_up_ref, w_down_ref, y_ref, wm_scr, bias_scr,
                   *, layer, sub_rows):
    @pl.when(pl.program_id(0) == 0)
    def _():
        _sgu_tables(w_sp_ref, b_sp_ref, wm_scr, bias_scr, None)

    _layer1_tile(x_ref, y_ref, None, wm_scr, bias_scr, _gains((g0, g1, g2, g3), layer), g_v_ref,
                 (w_uv_ref, w_out_ref, w_up_ref, w_down_ref), sub_rows, lambda k: None)


def _layer1_sample_kernel(x_ref, g0, g1, g2, g3, g_v_ref, w_sp_ref, b_sp_ref,
                          w_uv_hbm, w_out_hbm, w_up_hbm, w_down_hbm, y_ref, v_ref,
                          wm_scr, bias_scr, w_uv_ref, w_out_ref, w_up_ref, w_down_ref, sem,
                          *, layer, seq, sub_rows):
    @pl.when(pl.program_id(0) == 0)
    def _():
        _sgu_tables(w_sp_ref, b_sp_ref, wm_scr, bias_scr, seq)

    w_bufs = (w_uv_ref, w_out_ref, w_up_ref, w_down_ref)
    gains = _gains((g0, g1, g2, g3), layer)
    _lazy_weights((w_uv_hbm, w_out_hbm, w_up_hbm, w_down_hbm), w_bufs, sem,
                  lambda ready: _layer1_tile(x_ref, y_ref, v_ref, wm_scr, bias_scr, gains, g_v_ref,
                                             w_bufs, sub_rows, ready))


def _layer1_tile(x_ref, y_ref, v_ref, wm_scr, bias_scr, gains, g_v_ref, w_refs, sub_rows, ready):
    w_uv_ref, w_out_ref, w_up_ref, w_down_ref = w_refs
    tm = x_ref.shape[0]
    emit_v = v_ref is not None
    g_pre, g_post, gf_pre, gf_post = gains

    bias = bias_scr[...]
    for spans in _span_groups(tm, sub_rows):
        xs = [x_ref[pl.ds(r0, ts), :] for r0, ts in spans]
        hs = [_rms(x, g_pre).astype(BF16) for x in xs]
        ready(0)
        ps = [_dot(h, w_uv_ref[...]) for h in hs]

        gated = []
        for (r0, ts), p in zip(spans, ps):
            nc = ts // CHUNK
            u = p[:, :D_SGU]
            v = _rms(p[:, D_SGU:], g_v_ref[...])
            if emit_v:
                v_ref[pl.ds(r0, ts), :] = v
            vb = v.astype(BF16)
            cols = []
            for hh in range(SGU_HEADS):
                lanes = slice(hh * SGU_HD, (hh + 1) * SGU_HD)
                rhs = jnp.concatenate(
                    [vb[ci * CHUNK:(ci + 1) * CHUNK, lanes] for ci in range(nc)], axis=1)
                out = _dot(wm_scr[hh], rhs)
                cols.append(jnp.concatenate(
                    [out[:, ci * SGU_HD:(ci + 1) * SGU_HD] + bias[:, lanes] for ci in range(nc)],
                    axis=0))
            gated.append((u * jnp.concatenate(cols, axis=1)).astype(BF16))

        ready(1)
        x1 = [x + _rms(_dot(a, w_out_ref[...]), g_post) for x, a in zip(xs, gated)]
        ys = _ffn_staged(x1, gf_pre, gf_post, w_up_ref, w_down_ref, ready, 2)
        for (r0, ts), y in zip(spans, ys):
            y_ref[pl.ds(r0, ts), :] = y


def _resident(arr, layer=None):
    if layer is None:
        nd = arr.ndim
        return pl.BlockSpec(arr.shape, lambda *_: (0,) * nd, pipeline_mode=pl.Buffered(1))
    nd = arr.ndim - 1
    return pl.BlockSpec((None,) + arr.shape[1:], lambda *_: (layer,) + (0,) * nd,
                        pipeline_mode=pl.Buffered(1))


def _params(n_axes):
    return pltpu.CompilerParams(dimension_semantics=("arbitrary",) * n_axes,
                                vmem_limit_bytes=VMEM_LIMIT_BYTES)


def _split(weights):
    return [a for a, _ in weights], [_resident(a, l) for a, l in weights]


def _layer0_prompt(x, weights, big, layer, to_cast):
    b, s, d = x.shape
    tm = TILE_M_PROMPT
    nt = s // tm
    steps = b * nt
    arrs, specs = _split(weights)
    any_spec = pl.BlockSpec(memory_space=pl.ANY)
    big_shapes = [w.shape[1:] for w in big]
    tile = pl.BlockSpec((None, tm, d), lambda i, j: (i, j, 0))
    pool = pl.BlockSpec((None, None, POOL_PREV, D_POOL), lambda i, j: (0, i, 0, 0))
    conv = pl.BlockSpec((None, None, CONV_W - 1, D_CONV), lambda i, j: (0, i, 0, 0))
    cast_in, cast_out, cast_shapes = [], [], []
    for w, slab in to_cast:
        _, rows, cols = w.shape
        assert rows % steps == 0
        cast_in.append(pl.BlockSpec((None, rows // steps, cols),
                                    lambda i, j, slab=slab: (slab, i * nt + j, 0)))
        cast_out.append(pl.BlockSpec((rows // steps, cols), lambda i, j: (i * nt + j, 0)))
        cast_shapes.append(jax.ShapeDtypeStruct((rows, cols), BF16))
    return pl.pallas_call(
        functools.partial(_layer0_prompt_kernel, layer=layer, n_cast=len(to_cast)),
        grid=(b, nt),
        in_specs=[tile] + specs + [any_spec] * len(big) + cast_in,
        out_specs=[tile, pool, conv] + [any_spec] * len(big) + cast_out,
        out_shape=[jax.ShapeDtypeStruct(x.shape, F32),
                   jax.ShapeDtypeStruct((1, b, POOL_PREV, D_POOL), F32),
                   jax.ShapeDtypeStruct((1, b, CONV_W - 1, D_CONV), F32)]
        + [jax.ShapeDtypeStruct(sh, BF16) for sh in big_shapes] + cast_shapes,
        scratch_shapes=[pltpu.VMEM((HALO, D_POOL), F32), pltpu.VMEM((HALO, D_CONV), F32),
                        pltpu.VMEM((D_POOL + D_CONV, d), BF16)]
        + [pltpu.VMEM(sh, BF16) for sh in big_shapes]
        + [pltpu.VMEM((STAGE_SLOTS, STAGE_ROWS, STAGE_COLS), F32),
           pltpu.SemaphoreType.DMA((STAGE_SLOTS,)),
           pltpu.SemaphoreType.DMA((len(big),))],
        compiler_params=_params(2),
        name="layer0_prompt",
    )(x, *arrs, *big, *[w for w, _ in to_cast])


def _layer0_sample(x, state_pool, state_conv, weights, big, layer, seq):
    n, d = x.shape
    tm = TILE_M_SAMPLE
    tb = tm // seq
    arrs, specs = _split(weights)
    any_spec = pl.BlockSpec(memory_space=pl.ANY)
    tile = pl.BlockSpec((tm, d), lambda i: (i, 0))
    pool_t = jnp.swapaxes(state_pool, 1, 2)
    pool = pl.BlockSpec((None, POOL_PREV, tb, D_POOL), lambda i: (0, 0, i, 0))
    conv = pl.BlockSpec((None, tb, CONV_W - 1, D_CONV), lambda i: (0, i, 0, 0))
    y, pool_new, conv_new = pl.pallas_call(
        functools.partial(_layer0_sample_kernel, layer=layer, seq=seq),
        grid=(n // tm,),
        in_specs=[tile, pool, conv] + specs + [any_spec] * len(big),
        out_specs=[tile, pool, conv],
        out_shape=[jax.ShapeDtypeStruct(x.shape, F32),
                   jax.ShapeDtypeStruct(pool_t.shape, F32),
                   jax.ShapeDtypeStruct(state_conv.shape, F32)],
        scratch_shapes=[pltpu.VMEM((tb, HALO + seq, D_POOL), F32),
                        pltpu.VMEM((tb, HALO + seq, D_CONV), F32)]
        + [pltpu.VMEM(w.shape, BF16) for w in big] + [pltpu.SemaphoreType.DMA((len(big),))],
        compiler_params=_params(1),
        name="layer0_sample",
    )(x, pool_t, state_conv, *arrs, *big)
    return y, jnp.swapaxes(pool_new, 1, 2), conv_new


def _sgu_table_scratch():
    return [pltpu.VMEM((SGU_HEADS, CHUNK, CHUNK), BF16), pltpu.VMEM((CHUNK, D_SGU), F32)]


def _layer1_prompt(x, weights, big, layer):
    n, d = x.shape
    tm = TILE_M_PROMPT_L1
    arrs, specs = _split(tuple(weights) + tuple((w, None) for w in big))
    tile = pl.BlockSpec((tm, d), lambda i: (i, 0))
    return pl.pallas_call(
        functools.partial(_layer1_kernel, layer=layer, sub_rows=SUB_TILE_ROWS_L1),
        grid=(n // tm,),
        in_specs=[tile] + specs,
        out_specs=tile,
        out_shape=jax.ShapeDtypeStruct(x.shape, F32),
        scratch_shapes=_sgu_table_scratch(),
        compiler_params=_params(1),
        name="layer1_prompt",
    )(x, *arrs)


def _layer1_sample(x, weights, big, layer, seq):
    n, d = x.shape
    tm = TILE_M_SAMPLE
    arrs, specs = _split(weights)
    tile = pl.BlockSpec((tm, d), lambda i: (i, 0))
    return pl.pallas_call(
        functools.partial(_layer1_sample_kernel, layer=layer, seq=seq, sub_rows=SUB_TILE_ROWS),
        grid=(n // tm,),
        in_specs=[tile] + specs + [pl.BlockSpec(memory_space=pl.ANY)] * len(big),
        out_specs=[tile, tile],
        out_shape=[jax.ShapeDtypeStruct(x.shape, F32)] * 2,
        scratch_shapes=_sgu_table_scratch() + [pltpu.VMEM(w.shape, BF16) for w in big]
        + [pltpu.SemaphoreType.DMA((len(big),))],
        compiler_params=_params(1),
        name="layer1_sample",
    )(x, *arrs, *big)


def kernel(x_prompt, x_sample, state_pool, state_conv, g_mix_pre, g_mix_post, g_ffn_pre,
           g_ffn_post, w_in_ab, w_pool_grp, pool_scale, conv_w, w_out_ab, w_uv, g_v, w_spatial,
           b_spatial, w_out_c, w_up, w_down):
    bp, sp, d = x_prompt.shape
    bs, ss, _ = x_sample.shape
    assert ss < CHUNK and CHUNK % ss == 0 and CONV_W - 1 <= ss and sp % CHUNK == 0

    gains = ((g_mix_pre, None), (g_mix_post, None), (g_ffn_pre, None), (g_ffn_post, None))
    w0_small = gains + ((w_pool_grp, 0), (pool_scale, None), (conv_w, 0))

    (y0, pool_p, conv_p, w_in_b, w_eff_b, w_up0_b, w_down0_b,
     w_uv_b, w_out_c_b, w_up1_b, w_down1_b) = _layer0_prompt(
        x_prompt, w0_small, (w_in_ab, w_out_ab, w_up, w_down), 0,
        ((w_uv, 0), (w_out_c, 0), (w_up, 1), (w_down, 1)))
    w1_small = gains + ((g_v, None), (w_spatial, 0), (b_spatial, 0))
    w1_big = (w_uv_b, w_out_c_b, w_up1_b, w_down1_b)
    y_prompt = _layer1_prompt(y0.reshape(bp * sp, d), w1_small, w1_big, 1)

    ys0, pool_s, conv_s = _layer0_sample(x_sample.reshape(bs * ss, d), state_pool, state_conv,
                                         gains + ((conv_w, 0),),
                                         (w_in_b, w_eff_b, w_up0_b, w_down0_b), 0, ss)
    ys1, v_s = _layer1_sample(ys0, w1_small, w1_big, 1, ss)
    return (y_prompt.reshape(bp, sp, d), ys1.reshape(bs, ss, d), pool_p, pool_s, conv_p, conv_s,
            v_s.reshape(1, bs, ss, D_SGU))
```

```python
import functools

import jax
import jax.numpy as jnp
from jax import lax
from jax.experimental import pallas as pl
from jax.experimental.pallas import tpu as pltpu

D_MODEL = 1024
EPS = 1e-6
D_POOL = D_MODEL // 2
POOL_WINDOWS = (2, 4, 8, 16)
POOL_GC = D_POOL // len(POOL_WINDOWS)
POOL_PREV = max(POOL_WINDOWS) - 1
D_CONV = D_MODEL // 2
CONV_W = 3
D_SGU = D_MODEL
SGU_HEADS = 8
SGU_HD = D_SGU // SGU_HEADS
CHUNK = 128
PAST_LEN = 16384

HALO = 16
TILE_M_PROMPT = 1024
TILE_M_SAMPLE = 512
SUB_TILE_ROWS = 256
SUB_GROUP = 2
TILE_M_PROMPT_L1 = 1024
SUB_TILE_ROWS_L1 = 512
STAGE_ROWS, STAGE_COLS = 256, 1024
STAGE_SLOTS = 3
VMEM_LIMIT_BYTES = 63 * 1024 * 1024

F32 = jnp.float32
BF16 = jnp.bfloat16


def _rms(x, g):
    return x * lax.rsqrt(jnp.mean(x * x, axis=-1, keepdims=True) + EPS) * g


def _dot(a, b):
    return jnp.dot(a, b, preferred_element_type=F32)


def _shift_rows(a, k):
    return pltpu.roll(a, k, axis=0)


def _window_sums(full_u):
    sums = []
    for g, w in enumerate(POOL_WINDOWS):
        s = full_u[:, g * POOL_GC:(g + 1) * POOL_GC]
        k = 1
        while k < w:
            s = s + _shift_rows(s, k)
            k *= 2
        sums.append(s)
    return sums


def _pool_diff(sums, u, pos):
    outs = []
    for g, w in enumerate(POOL_WINDOWS):
        inv_cnt = 1.0 / jnp.minimum(pos + 1.0, float(w))
        outs.append(sums[g] * inv_cnt - u[:, g * POOL_GC:(g + 1) * POOL_GC])
    return jnp.concatenate(outs, axis=1)


def _fold_pool_into_out(w_grp_ref, scale_ref, w_eff_ref):
    scale = scale_ref[...]
    for g in range(len(POOL_WINDOWS)):
        rows = slice(g * POOL_GC, (g + 1) * POOL_GC)
        a = w_grp_ref[g] * scale[:, rows]
        a_hi = a.astype(BF16)
        a_lo = (a - a_hi.astype(F32)).astype(BF16)
        b = w_eff_ref[rows, :]
        w_eff_ref[rows, :] = (_dot(a_hi, b) + _dot(a_lo, b)).astype(BF16)


def _conv_rows(full_cx, conv_w):
    return (conv_w[0:1] * _shift_rows(full_cx, 2) + conv_w[1:2] * _shift_rows(full_cx, 1)
            + conv_w[2:3] * full_cx)


def _gains(gain_refs, layer):
    return [r[layer:layer + 1, :] for r in gain_refs]


def _in_proj(x, g_pre, w_in_ref):
    p = _dot(_rms(x, g_pre).astype(BF16), w_in_ref[...])
    u = p[:, :D_POOL]
    cx = p[:, D_POOL + 2 * D_CONV:] * p[:, D_POOL:D_POOL + D_CONV]
    gate_b = p[:, D_POOL + D_CONV:D_POOL + 2 * D_CONV]
    return u, cx, gate_b


def _span_groups(tm, sub_rows=SUB_TILE_ROWS):
    assert tm % sub_rows == 0
    spans = [(r0, sub_rows) for r0 in range(0, tm, sub_rows)]
    return [spans[k:k + SUB_GROUP] for k in range(0, len(spans), SUB_GROUP)]


def _ffn_staged(xs, g_pre, g_post, w_up_ref, w_down_ref):
    acts = []
    for x in xs:
        a = jnp.maximum(_dot(_rms(x, g_pre).astype(BF16), w_up_ref[...]), 0.0)
        acts.append((a * a).astype(BF16))
    return [x + _rms(_dot(a, w_down_ref[...]), g_post) for x, a in zip(xs, acts)]


def _stream_cast(srcs, slab, dsts, stage, sem):
    ahead = STAGE_SLOTS - 1

    def make_plan(src, dst, first):
        rows, cols = dst.shape
        assert rows % STAGE_ROWS == 0 and cols % STAGE_COLS == 0
        ncol = cols // STAGE_COLS
        n = (rows // STAGE_ROWS) * ncol
        assert n >= ahead

        def window(k):
            r0 = pl.multiple_of((k // ncol) * STAGE_ROWS, STAGE_ROWS)
            c0 = pl.multiple_of((k % ncol) * STAGE_COLS, STAGE_COLS)
            return pl.ds(r0, STAGE_ROWS), pl.ds(c0, STAGE_COLS)

        def fetch(k):
            rs, cs = window(k)
            slot = (first + k) % STAGE_SLOTS
            return pltpu.make_async_copy(src.at[slab, rs, cs], stage.at[slot], sem.at[slot])

        return n, window, fetch

    plans, first = [], 0
    for src, dst in zip(srcs, dsts):
        plans.append(make_plan(src, dst, first) + (first,))
        first += plans[-1][0]

    for k in range(ahead):
        plans[0][2](k).start()
    for j, ((n, window, fetch, first), dst) in enumerate(zip(plans, dsts)):
        next_fetch = plans[j + 1][2] if j + 1 < len(plans) else None

        def body(k, carry, n=n, window=window, fetch=fetch, first=first, dst=dst,
                 next_fetch=next_fetch):
            @pl.when(k + ahead < n)
            def _():
                fetch(k + ahead).start()

            if next_fetch is not None:
                @pl.when(k + ahead >= n)
                def _():
                    next_fetch(k + ahead - n).start()

            fetch(k).wait()
            rs, cs = window(k)
            dst[rs, cs] = stage[(first + k) % STAGE_SLOTS].astype(BF16)
            return carry

        lax.fori_loop(0, n, body, 0)


def _export_copies(bufs, outs, sem):
    return [pltpu.make_async_copy(buf, out, sem.at[j]) for j, (buf, out) in enumerate(zip(bufs, outs))]


def _layer0_prompt_kernel(x_ref, g0, g1, g2, g3, w_grp_ref, scale_ref, conv_w_ref,
                          *rest, layer, n_cast):
    w_hbm, cast_in, rest = rest[:4], rest[4:4 + n_cast], rest[4 + n_cast:]
    (y_ref, pool_out_ref, conv_out_ref), w_exp, rest = rest[:3], rest[3:7], rest[7:]
    cast_out, rest = rest[:n_cast], rest[n_cast:]
    halo_u, halo_cx, w_in_ref, w_eff_scr, w_up_ref, w_down_ref, stage, sem_in, sem_out = rest
    w_bufs = (w_in_ref, w_eff_scr, w_up_ref, w_down_ref)
    tm = x_ref.shape[0]
    t = pl.program_id(1)
    first = (pl.program_id(0) == 0) & (t == 0)
    last = (pl.program_id(0) == pl.num_programs(0) - 1) & (t == pl.num_programs(1) - 1)
    g_pre, g_post, gf_pre, gf_post = _gains((g0, g1, g2, g3), layer)

    for src, dst in zip(cast_in, cast_out):
        dst[...] = src[...].astype(BF16)

    @pl.when(first)
    def _():
        _stream_cast(w_hbm, layer // 2, w_bufs, stage, sem_in)
        _fold_pool_into_out(w_grp_ref, scale_ref, w_eff_scr)
        for c in _export_copies(w_bufs, w_exp, sem_out):
            c.start()

    @pl.when(last)
    def _():
        for c in _export_copies(w_bufs, w_exp, sem_out):
            c.wait()

    @pl.when(t == 0)
    def _():
        halo_u[...] = jnp.zeros_like(halo_u)
        halo_cx[...] = jnp.zeros_like(halo_cx)

    hist_u, hist_cx = halo_u[...], halo_cx[...]
    for spans in _span_groups(tm):
        xs = [x_ref[pl.ds(r0, ts), :] for r0, ts in spans]
        proj = [_in_proj(x, g_pre, w_in_ref) for x in xs]

        yabs = []
        for (r0, ts), (u, cx, gate_b) in zip(spans, proj):
            full_u = jnp.concatenate([hist_u, u], axis=0)
            full_cx = jnp.concatenate([hist_cx, cx], axis=0)
            hist_u, hist_cx = u[ts - HALO:], cx[ts - HALO:]
            pos = (lax.broadcasted_iota(jnp.int32, (ts, 1), 0) + (t * tm + r0)).astype(F32)
            sums = [s[HALO:] for s in _window_sums(full_u)]
            d = _pool_diff(sums, u, pos)
            cz = _conv_rows(full_cx, conv_w_ref[...])[HALO:]
            yabs.append(jnp.concatenate([d, gate_b * cz], axis=1).astype(BF16))

        x1 = [x + _rms(_dot(yab, w_eff_scr[...]), g_post) for x, yab in zip(xs, yabs)]
        for (r0, ts), y in zip(spans, _ffn_staged(x1, gf_pre, gf_post, w_up_ref, w_down_ref)):
            y_ref[pl.ds(r0, ts), :] = y
    halo_u[...] = hist_u
    halo_cx[...] = hist_cx

    pool_out_ref[...] = halo_u[HALO - POOL_PREV:, :]
    conv_out_ref[...] = halo_cx[HALO - (CONV_W - 1):, :]


def _layer0_sample_kernel(x_ref, pool_ref, conv_ref, g0, g1, g2, g3, conv_w_ref,
                          w_in_ref, w_eff_ref, w_up_ref, w_down_ref,
                          y_ref, pool_out_ref, conv_out_ref, full_u_scr, full_cx_scr,
                          *, layer, seq):
    tm = x_ref.shape[0]
    grp = HALO + seq
    g_pre, g_post, gf_pre, gf_post = _gains((g0, g1, g2, g3), layer)

    (spans,) = _span_groups(tm, tm // SUB_GROUP)
    ts = spans[0][1]
    tbs = ts // seq
    xs = [x_ref[pl.ds(r0, ts), :] for r0, ts in spans]
    proj = [_in_proj(x, g_pre, w_in_ref) for x in xs]

    def new_rows(a):
        return a.reshape(tbs, grp, a.shape[-1])[:, HALO:].reshape(ts, a.shape[-1])

    pos = (lax.broadcasted_iota(jnp.int32, (tbs, seq, 1), 1).reshape(ts, 1) + PAST_LEN).astype(F32)
    yabs = []
    for (r0, _), (u, cx, gate_b) in zip(spans, proj):
        sq = pl.ds(r0 // seq, tbs)
        full_u_scr[sq, :HALO - POOL_PREV, :] = jnp.zeros((tbs, HALO - POOL_PREV, D_POOL), F32)
        for r in range(POOL_PREV):
            full_u_scr[sq, HALO - POOL_PREV + r, :] = pool_ref[r, sq, :]
        full_u_scr[sq, HALO:, :] = u.reshape(tbs, seq, D_POOL)
        full_cx_scr[sq, :HALO - (CONV_W - 1), :] = jnp.zeros((tbs, HALO - (CONV_W - 1), D_CONV), F32)
        full_cx_scr[sq, HALO - (CONV_W - 1):HALO, :] = conv_ref[sq]
        full_cx_scr[sq, HALO:, :] = cx.reshape(tbs, seq, D_CONV)
        for r in range(POOL_PREV):
            pool_out_ref[r, sq, :] = full_u_scr[sq, grp - POOL_PREV + r, :]
        conv_out_ref[sq] = full_cx_scr[sq, grp - (CONV_W - 1):, :]
        full_u = full_u_scr[sq].reshape(tbs * grp, D_POOL)
        full_cx = full_cx_scr[sq].reshape(tbs * grp, D_CONV)
        sums = [new_rows(s) for s in _window_sums(full_u)]
        d = _pool_diff(sums, u, pos)
        cz = new_rows(_conv_rows(full_cx, conv_w_ref[...]))
        yabs.append(jnp.concatenate([d, gate_b * cz], axis=1).astype(BF16))

    x1 = [x + _rms(_dot(yab, w_eff_ref[...]), g_post) for x, yab in zip(xs, yabs)]
    for (r0, _), y in zip(spans, _ffn_staged(x1, gf_pre, gf_post, w_up_ref, w_down_ref)):
        y_ref[pl.ds(r0, ts), :] = y


def _sgu_tables(w_sp_ref, b_sp_ref, wm_scr, bias_scr, seq):
    r = lax.broadcasted_iota(jnp.int32, (CHUNK, CHUNK), 0)
    c = lax.broadcasted_iota(jnp.int32, (CHUNK, CHUNK), 1)
    keep = r >= c
    b = b_sp_ref[...]
    if seq is not None:
        keep = keep & (r // seq == c // seq)
        pick = (r % seq == c).astype(BF16)
        pick_t = (c % seq == r).astype(BF16)
        b = jnp.where(lax.broadcasted_iota(jnp.int32, b.shape, 1) < seq, b, 0.0)
        k = seq
        while k < CHUNK:
            b = b + pltpu.roll(b, k, axis=1)
            k *= 2
    mask = keep.astype(F32)
    for hh in range(SGU_HEADS):
        w = w_sp_ref[hh]
        if seq is not None:
            w = _dot(_dot(pick, w.astype(BF16)).astype(BF16), pick_t)
        wm_scr[hh] = (w * mask).astype(BF16)
        rows = jnp.broadcast_to(b[hh:hh + 1, :], (CHUNK, SGU_HD))
        bias_scr[:, hh * SGU_HD:(hh + 1) * SGU_HD] = rows.T


def _layer1_kernel(x_ref, g0, g1, g2, g3, g_v_ref, w_sp_ref, b_sp_ref,
                   w_uv_ref, w_out_ref, w_up_ref, w_down_ref, y_ref, *rest, layer, seq, emit_v,
                   sub_rows):
    if emit_v:
        v_ref, wm_scr, bias_scr = rest
    else:
        wm_scr, bias_scr = rest
    tm = x_ref.shape[0]
    g_pre, g_post, gf_pre, gf_post = _gains((g0, g1, g2, g3), layer)

    @pl.when(pl.program_id(0) == 0)
    def _():
        _sgu_tables(w_sp_ref, b_sp_ref, wm_scr, bias_scr, seq)

    bias = bias_scr[...]
    for spans in _span_groups(tm, sub_rows):
        xs = [x_ref[pl.ds(r0, ts), :] for r0, ts in spans]
        ps = [_dot(_rms(x, g_pre).astype(BF16), w_uv_ref[...]) for x in xs]

        gated = []
        for (r0, ts), p in zip(spans, ps):
            nc = ts // CHUNK
            u = p[:, :D_SGU]
            v = _rms(p[:, D_SGU:], g_v_ref[...])
            if emit_v:
                v_ref[pl.ds(r0, ts), :] = v
            vb = v.astype(BF16)
            cols = []
            for hh in range(SGU_HEADS):
                lanes = slice(hh * SGU_HD, (hh + 1) * SGU_HD)
                rhs = jnp.concatenate(
                    [vb[ci * CHUNK:(ci + 1) * CHUNK, lanes] for ci in range(nc)], axis=1)
                out = _dot(wm_scr[hh], rhs)
                cols.append(jnp.concatenate(
                    [out[:, ci * SGU_HD:(ci + 1) * SGU_HD] + bias[:, lanes] for ci in range(nc)],
                    axis=0))
            gated.append((u * jnp.concatenate(cols, axis=1)).astype(BF16))

        x1 = [x + _rms(_dot(a, w_out_ref[...]), g_post) for x, a in zip(xs, gated)]
        for (r0, ts), y in zip(spans, _ffn_staged(x1, gf_pre, gf_post, w_up_ref, w_down_ref)):
            y_ref[pl.ds(r0, ts), :] = y


def _resident(arr, layer=None):
    if layer is None:
        nd = arr.ndim
        return pl.BlockSpec(arr.shape, lambda *_: (0,) * nd, pipeline_mode=pl.Buffered(1))
    nd = arr.ndim - 1
    return pl.BlockSpec((None,) + arr.shape[1:], lambda *_: (layer,) + (0,) * nd,
                        pipeline_mode=pl.Buffered(1))


def _params(n_axes):
    return pltpu.CompilerParams(dimension_semantics=("arbitrary",) * n_axes,
                                vmem_limit_bytes=VMEM_LIMIT_BYTES)


def _split(weights):
    return [a for a, _ in weights], [_resident(a, l) for a, l in weights]


def _layer0_prompt(x, weights, big, layer, to_cast):
    b, s, d = x.shape
    tm = TILE_M_PROMPT
    nt = s // tm
    steps = b * nt
    arrs, specs = _split(weights)
    any_spec = pl.BlockSpec(memory_space=pl.ANY)
    big_shapes = [w.shape[1:] for w in big]
    tile = pl.BlockSpec((None, tm, d), lambda i, j: (i, j, 0))
    pool = pl.BlockSpec((None, None, POOL_PREV, D_POOL), lambda i, j: (0, i, 0, 0))
    conv = pl.BlockSpec((None, None, CONV_W - 1, D_CONV), lambda i, j: (0, i, 0, 0))
    cast_in, cast_out, cast_shapes = [], [], []
    for w, slab in to_cast:
        _, rows, cols = w.shape
        assert rows % steps == 0
        cast_in.append(pl.BlockSpec((None, rows // steps, cols),
                                    lambda i, j, slab=slab: (slab, i * nt + j, 0)))
        cast_out.append(pl.BlockSpec((rows // steps, cols), lambda i, j: (i * nt + j, 0)))
        cast_shapes.append(jax.ShapeDtypeStruct((rows, cols), BF16))
    return pl.pallas_call(
        functools.partial(_layer0_prompt_kernel, layer=layer, n_cast=len(to_cast)),
        grid=(b, nt),
        in_specs=[tile] + specs + [any_spec] * len(big) + cast_in,
        out_specs=[tile, pool, conv] + [any_spec] * len(big) + cast_out,
        out_shape=[jax.ShapeDtypeStruct(x.shape, F32),
                   jax.ShapeDtypeStruct((1, b, POOL_PREV, D_POOL), F32),
                   jax.ShapeDtypeStruct((1, b, CONV_W - 1, D_CONV), F32)]
        + [jax.ShapeDtypeStruct(sh, BF16) for sh in big_shapes] + cast_shapes,
        scratch_shapes=[pltpu.VMEM((HALO, D_POOL), F32), pltpu.VMEM((HALO, D_CONV), F32)]
        + [pltpu.VMEM(sh, BF16) for sh in big_shapes]
        + [pltpu.VMEM((STAGE_SLOTS, STAGE_ROWS, STAGE_COLS), F32),
           pltpu.SemaphoreType.DMA((STAGE_SLOTS,)),
           pltpu.SemaphoreType.DMA((len(big),))],
        compiler_params=_params(2),
        name="layer0_prompt",
    )(x, *arrs, *big, *[w for w, _ in to_cast])


def _layer0_sample(x, state_pool, state_conv, weights, layer, seq):
    n, d = x.shape
    tm = TILE_M_SAMPLE
    tb = tm // seq
    arrs, specs = _split(weights)
    tile = pl.BlockSpec((tm, d), lambda i: (i, 0))
    pool_t = jnp.swapaxes(state_pool, 1, 2)
    pool = pl.BlockSpec((None, POOL_PREV, tb, D_POOL), lambda i: (0, 0, i, 0))
    conv = pl.BlockSpec((None, tb, CONV_W - 1, D_CONV), lambda i: (0, i, 0, 0))
    y, pool_new, conv_new = pl.pallas_call(
        functools.partial(_layer0_sample_kernel, layer=layer, seq=seq),
        grid=(n // tm,),
        in_specs=[tile, pool, conv] + specs,
        out_specs=[tile, pool, conv],
        out_shape=[jax.ShapeDtypeStruct(x.shape, F32),
                   jax.ShapeDtypeStruct(pool_t.shape, F32),
                   jax.ShapeDtypeStruct(state_conv.shape, F32)],
        scratch_shapes=[pltpu.VMEM((tb, HALO + seq, D_POOL), F32),
                        pltpu.VMEM((tb, HALO + seq, D_CONV), F32)],
        compiler_params=_params(1),
        name="layer0_sample",
    )(x, pool_t, state_conv, *arrs)
    return y, jnp.swapaxes(pool_new, 1, 2), conv_new


def _layer1(x, weights, tm, sub_rows, layer, seq, emit_v, name):
    n, d = x.shape
    arrs, specs = _split(weights)
    tile = pl.BlockSpec((tm, d), lambda i: (i, 0))
    out_specs = [tile, tile] if emit_v else [tile]
    out_shape = [jax.ShapeDtypeStruct(x.shape, F32)] * len(out_specs)
    return pl.pallas_call(
        functools.partial(_layer1_kernel, layer=layer, seq=seq, emit_v=emit_v,
                          sub_rows=sub_rows),
        grid=(n // tm,),
        in_specs=[tile] + specs,
        out_specs=out_specs,
        out_shape=out_shape,
        scratch_shapes=[pltpu.VMEM((SGU_HEADS, CHUNK, CHUNK), BF16),
                        pltpu.VMEM((CHUNK, D_SGU), F32)],
        compiler_params=_params(1),
        name=name,
    )(x, *arrs)


def kernel(x_prompt, x_sample, state_pool, state_conv, g_mix_pre, g_mix_post, g_ffn_pre,
           g_ffn_post, w_in_ab, w_pool_grp, pool_scale, conv_w, w_out_ab, w_uv, g_v, w_spatial,
           b_spatial, w_out_c, w_up, w_down):
    bp, sp, d = x_prompt.shape
    bs, ss, _ = x_sample.shape
    assert ss < CHUNK and CHUNK % ss == 0 and CONV_W - 1 <= ss and sp % CHUNK == 0

    gains = ((g_mix_pre, None), (g_mix_post, None), (g_ffn_pre, None), (g_ffn_post, None))
    w0_small = gains + ((w_pool_grp, 0), (pool_scale, None), (conv_w, 0))

    (y0, pool_p, conv_p, w_in_b, w_eff_b, w_up0_b, w_down0_b,
     w_uv_b, w_out_c_b, w_up1_b, w_down1_b) = _layer0_prompt(
        x_prompt, w0_small, (w_in_ab, w_out_ab, w_up, w_down), 0,
        ((w_uv, 0), (w_out_c, 0), (w_up, 1), (w_down, 1)))
    w0 = gains + ((conv_w, 0), (w_in_b, None), (w_eff_b, None), (w_up0_b, None), (w_down0_b, None))
    w1 = gains + ((g_v, None), (w_spatial, 0), (b_spatial, 0),
                  (w_uv_b, None), (w_out_c_b, None), (w_up1_b, None), (w_down1_b, None))
    (y_prompt,) = _layer1(y0.reshape(bp * sp, d), w1, TILE_M_PROMPT_L1, SUB_TILE_ROWS_L1, 1, None,
                          False, "layer1_prompt")

    ys0, pool_s, conv_s = _layer0_sample(x_sample.reshape(bs * ss, d), state_pool, state_conv,
                                         w0, 0, ss)
    ys1, v_s = _layer1(ys0, w1, TILE_M_SAMPLE, SUB_TILE_ROWS, 1, ss, True, "layer1_sample")
    return (y_prompt.reshape(bp, sp, d), ys1.reshape(bs, ss, d), pool_p, pool_s, conv_p, conv_s,
            v_s.reshape(1, bs, ss, D_SGU))
```

```python
import functools

import jax
import jax.numpy as jnp
from jax import lax
from jax.experimental import pallas as pl
from jax.experimental.pallas import tpu as pltpu

D_MODEL = 1024
EPS = 1e-6
D_POOL = D_MODEL // 2
POOL_WINDOWS = (2, 4, 8, 16)
POOL_GC = D_POOL // len(POOL_WINDOWS)
POOL_PREV = max(POOL_WINDOWS) - 1
D_CONV = D_MODEL // 2
CONV_W = 3
D_SGU = D_MODEL
SGU_HEADS = 8
SGU_HD = D_SGU // SGU_HEADS
CHUNK = 128
PAST_LEN = 16384

HALO = 16
TILE_M_PROMPT = 1024
TILE_M_SAMPLE = 512
SUB_TILE_ROWS = 256
SUB_GROUP = 2
TILE_M_PROMPT_L1 = 1024
SUB_TILE_ROWS_L1 = 512
FFN_SLICE = 1024
STAGE_ROWS, STAGE_COLS = 256, 1024
STAGE_SLOTS = 3
VMEM_LIMIT_BYTES = 63 * 1024 * 1024

F32 = jnp.float32
BF16 = jnp.bfloat16


def _rms(x, g):
    return x * lax.rsqrt(jnp.mean(x * x, axis=-1, keepdims=True) + EPS) * g


def _dot(a, b):
    return jnp.dot(a, b, preferred_element_type=F32)


def _shift_rows(a, k):
    return pltpu.roll(a, k, axis=0)


def _window_sums(full_u):
    sums = []
    for g, w in enumerate(POOL_WINDOWS):
        s = full_u[:, g * POOL_GC:(g + 1) * POOL_GC]
        k = 1
        while k < w:
            s = s + _shift_rows(s, k)
            k *= 2
        sums.append(s)
    return sums


def _pool_diff(sums, u, pos):
    outs = []
    for g, w in enumerate(POOL_WINDOWS):
        inv_cnt = 1.0 / jnp.minimum(pos + 1.0, float(w))
        outs.append(sums[g] * inv_cnt - u[:, g * POOL_GC:(g + 1) * POOL_GC])
    return jnp.concatenate(outs, axis=1)


def _fold_pool_into_out(w_grp_ref, scale_ref, w_eff_ref):
    scale = scale_ref[...]
    for g in range(len(POOL_WINDOWS)):
        rows = slice(g * POOL_GC, (g + 1) * POOL_GC)
        a = w_grp_ref[g] * scale[:, rows]
        a_hi = a.astype(BF16)
        a_lo = (a - a_hi.astype(F32)).astype(BF16)
        b = w_eff_ref[rows, :]
        w_eff_ref[rows, :] = (_dot(a_hi, b) + _dot(a_lo, b)).astype(BF16)


def _conv_rows(full_cx, conv_w):
    return (conv_w[0:1] * _shift_rows(full_cx, 2) + conv_w[1:2] * _shift_rows(full_cx, 1)
            + conv_w[2:3] * full_cx)


def _gains(gain_refs, layer):
    return [r[layer:layer + 1, :] for r in gain_refs]


def _in_proj(x, g_pre, w_in_ref):
    p = _dot(_rms(x, g_pre).astype(BF16), w_in_ref[...])
    u = p[:, :D_POOL]
    cx = p[:, D_POOL + 2 * D_CONV:] * p[:, D_POOL:D_POOL + D_CONV]
    gate_b = p[:, D_POOL + D_CONV:D_POOL + 2 * D_CONV]
    return u, cx, gate_b


def _span_groups(tm, sub_rows=SUB_TILE_ROWS):
    assert tm % sub_rows == 0
    spans = [(r0, sub_rows) for r0 in range(0, tm, sub_rows)]
    return [spans[k:k + SUB_GROUP] for k in range(0, len(spans), SUB_GROUP)]


def _ffn_staged(xs, g_pre, g_post, w_up_ref, w_down_ref):
    acts = []
    for x in xs:
        a = jnp.maximum(_dot(_rms(x, g_pre).astype(BF16), w_up_ref[...]), 0.0)
        acts.append((a * a).astype(BF16))
    return [x + _rms(_dot(a, w_down_ref[...]), g_post) for x, a in zip(xs, acts)]


def _ffn_scratch(xs, spans, y_ref, act_scr, g_pre, g_post, w_up_ref, w_down_ref):
    hs = [_rms(x, g_pre).astype(BF16) for x in xs]
    for (r0, ts), h in zip(spans, hs):
        for c0 in range(0, w_up_ref.shape[1], FFN_SLICE):
            a = jnp.maximum(_dot(h, w_up_ref[:, c0:c0 + FFN_SLICE]), 0.0)
            act_scr[pl.ds(r0, ts), c0:c0 + FFN_SLICE] = (a * a).astype(BF16)
    for (r0, ts), x in zip(spans, xs):
        f = _dot(act_scr[pl.ds(r0, ts), :], w_down_ref[...])
        y_ref[pl.ds(r0, ts), :] = x + _rms(f, g_post)


def _stream_cast(srcs, slab, dsts, stage, sem):
    ahead = STAGE_SLOTS - 1

    def make_plan(src, dst, first):
        rows, cols = dst.shape
        assert rows % STAGE_ROWS == 0 and cols % STAGE_COLS == 0
        ncol = cols // STAGE_COLS
        n = (rows // STAGE_ROWS) * ncol
        assert n >= ahead

        def window(k):
            r0 = pl.multiple_of((k // ncol) * STAGE_ROWS, STAGE_ROWS)
            c0 = pl.multiple_of((k % ncol) * STAGE_COLS, STAGE_COLS)
            return pl.ds(r0, STAGE_ROWS), pl.ds(c0, STAGE_COLS)

        def fetch(k):
            rs, cs = window(k)
            slot = (first + k) % STAGE_SLOTS
            return pltpu.make_async_copy(src.at[slab, rs, cs], stage.at[slot], sem.at[slot])

        return n, window, fetch

    plans, first = [], 0
    for src, dst in zip(srcs, dsts):
        plans.append(make_plan(src, dst, first) + (first,))
        first += plans[-1][0]

    for k in range(ahead):
        plans[0][2](k).start()
    for j, ((n, window, fetch, first), dst) in enumerate(zip(plans, dsts)):
        next_fetch = plans[j + 1][2] if j + 1 < len(plans) else None

        def body(k, carry, n=n, window=window, fetch=fetch, first=first, dst=dst,
                 next_fetch=next_fetch):
            @pl.when(k + ahead < n)
            def _():
                fetch(k + ahead).start()

            if next_fetch is not None:
                @pl.when(k + ahead >= n)
                def _():
                    next_fetch(k + ahead - n).start()

            fetch(k).wait()
            rs, cs = window(k)
            dst[rs, cs] = stage[(first + k) % STAGE_SLOTS].astype(BF16)
            return carry

        lax.fori_loop(0, n, body, 0)


def _export_copies(bufs, outs, sem):
    return [pltpu.make_async_copy(buf, out, sem.at[j]) for j, (buf, out) in enumerate(zip(bufs, outs))]


def _layer0_prompt_kernel(x_ref, g0, g1, g2, g3, w_grp_ref, scale_ref, conv_w_ref,
                          *rest, layer, n_cast):
    w_hbm, cast_in, rest = rest[:4], rest[4:4 + n_cast], rest[4 + n_cast:]
    (y_ref, pool_out_ref, conv_out_ref), w_exp, rest = rest[:3], rest[3:7], rest[7:]
    cast_out, rest = rest[:n_cast], rest[n_cast:]
    (halo_u, halo_cx, act_scr, w_in_ref, w_eff_scr, w_up_ref, w_down_ref, stage, sem_in,
     sem_out) = rest
    w_bufs = (w_in_ref, w_eff_scr, w_up_ref, w_down_ref)
    tm = x_ref.shape[0]
    t = pl.program_id(1)
    first = (pl.program_id(0) == 0) & (t == 0)
    last = (pl.program_id(0) == pl.num_programs(0) - 1) & (t == pl.num_programs(1) - 1)
    g_pre, g_post, gf_pre, gf_post = _gains((g0, g1, g2, g3), layer)

    for src, dst in zip(cast_in, cast_out):
        dst[...] = src[...].astype(BF16)

    @pl.when(first)
    def _():
        _stream_cast(w_hbm, layer // 2, w_bufs, stage, sem_in)
        _fold_pool_into_out(w_grp_ref, scale_ref, w_eff_scr)
        for c in _export_copies(w_bufs, w_exp, sem_out):
            c.start()

    @pl.when(last)
    def _():
        for c in _export_copies(w_bufs, w_exp, sem_out):
            c.wait()

    @pl.when(t == 0)
    def _():
        halo_u[...] = jnp.zeros_like(halo_u)
        halo_cx[...] = jnp.zeros_like(halo_cx)

    hist_u, hist_cx = halo_u[...], halo_cx[...]
    for spans in _span_groups(tm, SUB_TILE_ROWS_L1):
        xs = [x_ref[pl.ds(r0, ts), :] for r0, ts in spans]
        proj = [_in_proj(x, g_pre, w_in_ref) for x in xs]

        yabs = []
        for (r0, ts), (u, cx, gate_b) in zip(spans, proj):
            full_u = jnp.concatenate([hist_u, u], axis=0)
            full_cx = jnp.concatenate([hist_cx, cx], axis=0)
            hist_u, hist_cx = u[ts - HALO:], cx[ts - HALO:]
            pos = (lax.broadcasted_iota(jnp.int32, (ts, 1), 0) + (t * tm + r0)).astype(F32)
            sums = [s[HALO:] for s in _window_sums(full_u)]
            d = _pool_diff(sums, u, pos)
            cz = _conv_rows(full_cx, conv_w_ref[...])[HALO:]
            yabs.append(jnp.concatenate([d, gate_b * cz], axis=1).astype(BF16))

        x1 = [x + _rms(_dot(yab, w_eff_scr[...]), g_post) for x, yab in zip(xs, yabs)]
        _ffn_scratch(x1, spans, y_ref, act_scr, gf_pre, gf_post, w_up_ref, w_down_ref)
    halo_u[...] = hist_u
    halo_cx[...] = hist_cx

    pool_out_ref[...] = halo_u[HALO - POOL_PREV:, :]
    conv_out_ref[...] = halo_cx[HALO - (CONV_W - 1):, :]


def _layer0_sample_kernel(x_ref, pool_ref, conv_ref, g0, g1, g2, g3, conv_w_ref,
                          w_in_ref, w_eff_ref, w_up_ref, w_down_ref,
                          y_ref, pool_out_ref, conv_out_ref, full_u_scr, full_cx_scr,
                          *, layer, seq):
    tm = x_ref.shape[0]
    grp = HALO + seq
    g_pre, g_post, gf_pre, gf_post = _gains((g0, g1, g2, g3), layer)

    (spans,) = _span_groups(tm, tm // SUB_GROUP)
    ts = spans[0][1]
    tbs = ts // seq
    xs = [x_ref[pl.ds(r0, ts), :] for r0, ts in spans]
    proj = [_in_proj(x, g_pre, w_in_ref) for x in xs]

    def new_rows(a):
        return a.reshape(tbs, grp, a.shape[-1])[:, HALO:].reshape(ts, a.shape[-1])

    pos = (lax.broadcasted_iota(jnp.int32, (tbs, seq, 1), 1).reshape(ts, 1) + PAST_LEN).astype(F32)
    yabs = []
    for (r0, _), (u, cx, gate_b) in zip(spans, proj):
        sq = pl.ds(r0 // seq, tbs)
        full_u_scr[sq, :HALO - POOL_PREV, :] = jnp.zeros((tbs, HALO - POOL_PREV, D_POOL), F32)
        for r in range(POOL_PREV):
            full_u_scr[sq, HALO - POOL_PREV + r, :] = pool_ref[r, sq, :]
        full_u_scr[sq, HALO:, :] = u.reshape(tbs, seq, D_POOL)
        full_cx_scr[sq, :HALO - (CONV_W - 1), :] = jnp.zeros((tbs, HALO - (CONV_W - 1), D_CONV), F32)
        full_cx_scr[sq, HALO - (CONV_W - 1):HALO, :] = conv_ref[sq]
        full_cx_scr[sq, HALO:, :] = cx.reshape(tbs, seq, D_CONV)
        for r in range(POOL_PREV):
            pool_out_ref[r, sq, :] = full_u_scr[sq, grp - POOL_PREV + r, :]
        conv_out_ref[sq] = full_cx_scr[sq, grp - (CONV_W - 1):, :]
        full_u = full_u_scr[sq].reshape(tbs * grp, D_POOL)
        full_cx = full_cx_scr[sq].reshape(tbs * grp, D_CONV)
        sums = [new_rows(s) for s in _window_sums(full_u)]
        d = _pool_diff(sums, u, pos)
        cz = new_rows(_conv_rows(full_cx, conv_w_ref[...]))
        yabs.append(jnp.concatenate([d, gate_b * cz], axis=1).astype(BF16))

    x1 = [x + _rms(_dot(yab, w_eff_ref[...]), g_post) for x, yab in zip(xs, yabs)]
    for (r0, _), y in zip(spans, _ffn_staged(x1, gf_pre, gf_post, w_up_ref, w_down_ref)):
        y_ref[pl.ds(r0, ts), :] = y


def _sgu_tables(w_sp_ref, b_sp_ref, wm_scr, bias_scr, seq):
    r = lax.broadcasted_iota(jnp.int32, (CHUNK, CHUNK), 0)
    c = lax.broadcasted_iota(jnp.int32, (CHUNK, CHUNK), 1)
    keep = r >= c
    b = b_sp_ref[...]
    if seq is not None:
        keep = keep & (r // seq == c // seq)
        pick = (r % seq == c).astype(BF16)
        pick_t = (c % seq == r).astype(BF16)
        b = jnp.where(lax.broadcasted_iota(jnp.int32, b.shape, 1) < seq, b, 0.0)
        k = seq
        while k < CHUNK:
            b = b + pltpu.roll(b, k, axis=1)
            k *= 2
    mask = keep.astype(F32)
    for hh in range(SGU_HEADS):
        w = w_sp_ref[hh]
        if seq is not None:
            w = _dot(_dot(pick, w.astype(BF16)).astype(BF16), pick_t)
        wm_scr[hh] = (w * mask).astype(BF16)
        rows = jnp.broadcast_to(b[hh:hh + 1, :], (CHUNK, SGU_HD))
        bias_scr[:, hh * SGU_HD:(hh + 1) * SGU_HD] = rows.T


def _layer1_kernel(x_ref, g0, g1, g2, g3, g_v_ref, w_sp_ref, b_sp_ref,
                   w_uv_ref, w_out_ref, w_up_ref, w_down_ref, y_ref, *rest, layer, seq, emit_v,
                   sub_rows):
    if emit_v:
        v_ref, wm_scr, bias_scr, act_scr = rest
    else:
        wm_scr, bias_scr, act_scr = rest
    tm = x_ref.shape[0]
    g_pre, g_post, gf_pre, gf_post = _gains((g0, g1, g2, g3), layer)

    @pl.when(pl.program_id(0) == 0)
    def _():
        _sgu_tables(w_sp_ref, b_sp_ref, wm_scr, bias_scr, seq)

    bias = bias_scr[...]
    for spans in _span_groups(tm, sub_rows):
        xs = [x_ref[pl.ds(r0, ts), :] for r0, ts in spans]
        ps = [_dot(_rms(x, g_pre).astype(BF16), w_uv_ref[...]) for x in xs]

        gated = []
        for (r0, ts), p in zip(spans, ps):
            nc = ts // CHUNK
            u = p[:, :D_SGU]
            v = _rms(p[:, D_SGU:], g_v_ref[...])
            if emit_v:
                v_ref[pl.ds(r0, ts), :] = v
            vb = v.astype(BF16)
            cols = []
            for hh in range(SGU_HEADS):
                lanes = slice(hh * SGU_HD, (hh + 1) * SGU_HD)
                rhs = jnp.concatenate(
                    [vb[ci * CHUNK:(ci + 1) * CHUNK, lanes] for ci in range(nc)], axis=1)
                out = _dot(wm_scr[hh], rhs)
                cols.append(jnp.concatenate(
                    [out[:, ci * SGU_HD:(ci + 1) * SGU_HD] + bias[:, lanes] for ci in range(nc)],
                    axis=0))
            gated.append((u * jnp.concatenate(cols, axis=1)).astype(BF16))

        x1 = [x + _rms(_dot(a, w_out_ref[...]), g_post) for x, a in zip(xs, gated)]
        _ffn_scratch(x1, spans, y_ref, act_scr, gf_pre, gf_post, w_up_ref, w_down_ref)


def _resident(arr, layer=None):
    if layer is None:
        nd = arr.ndim
        return pl.BlockSpec(arr.shape, lambda *_: (0,) * nd, pipeline_mode=pl.Buffered(1))
    nd = arr.ndim - 1
    return pl.BlockSpec((None,) + arr.shape[1:], lambda *_: (layer,) + (0,) * nd,
                        pipeline_mode=pl.Buffered(1))


def _params(n_axes):
    return pltpu.CompilerParams(dimension_semantics=("arbitrary",) * n_axes,
                                vmem_limit_bytes=VMEM_LIMIT_BYTES)


def _split(weights):
    return [a for a, _ in weights], [_resident(a, l) for a, l in weights]


def _layer0_prompt(x, weights, big, layer, to_cast):
    b, s, d = x.shape
    tm = TILE_M_PROMPT
    nt = s // tm
    steps = b * nt
    arrs, specs = _split(weights)
    any_spec = pl.BlockSpec(memory_space=pl.ANY)
    big_shapes = [w.shape[1:] for w in big]
    tile = pl.BlockSpec((None, tm, d), lambda i, j: (i, j, 0))
    pool = pl.BlockSpec((None, None, POOL_PREV, D_POOL), lambda i, j: (0, i, 0, 0))
    conv = pl.BlockSpec((None, None, CONV_W - 1, D_CONV), lambda i, j: (0, i, 0, 0))
    cast_in, cast_out, cast_shapes = [], [], []
    for w, slab in to_cast:
        _, rows, cols = w.shape
        assert rows % steps == 0
        cast_in.append(pl.BlockSpec((None, rows // steps, cols),
                                    lambda i, j, slab=slab: (slab, i * nt + j, 0)))
        cast_out.append(pl.BlockSpec((rows // steps, cols), lambda i, j: (i * nt + j, 0)))
        cast_shapes.append(jax.ShapeDtypeStruct((rows, cols), BF16))
    return pl.pallas_call(
        functools.partial(_layer0_prompt_kernel, layer=layer, n_cast=len(to_cast)),
        grid=(b, nt),
        in_specs=[tile] + specs + [any_spec] * len(big) + cast_in,
        out_specs=[tile, pool, conv] + [any_spec] * len(big) + cast_out,
        out_shape=[jax.ShapeDtypeStruct(x.shape, F32),
                   jax.ShapeDtypeStruct((1, b, POOL_PREV, D_POOL), F32),
                   jax.ShapeDtypeStruct((1, b, CONV_W - 1, D_CONV), F32)]
        + [jax.ShapeDtypeStruct(sh, BF16) for sh in big_shapes] + cast_shapes,
        scratch_shapes=[pltpu.VMEM((HALO, D_POOL), F32), pltpu.VMEM((HALO, D_CONV), F32),
                        pltpu.VMEM((tm, big_shapes[2][1]), BF16)]
        + [pltpu.VMEM(sh, BF16) for sh in big_shapes]
        + [pltpu.VMEM((STAGE_SLOTS, STAGE_ROWS, STAGE_COLS), F32),
           pltpu.SemaphoreType.DMA((STAGE_SLOTS,)),
           pltpu.SemaphoreType.DMA((len(big),))],
        compiler_params=_params(2),
        name="layer0_prompt",
    )(x, *arrs, *big, *[w for w, _ in to_cast])


def _layer0_sample(x, state_pool, state_conv, weights, layer, seq):
    n, d = x.shape
    tm = TILE_M_SAMPLE
    tb = tm // seq
    arrs, specs = _split(weights)
    tile = pl.BlockSpec((tm, d), lambda i: (i, 0))
    pool_t = jnp.swapaxes(state_pool, 1, 2)
    pool = pl.BlockSpec((None, POOL_PREV, tb, D_POOL), lambda i: (0, 0, i, 0))
    conv = pl.BlockSpec((None, tb, CONV_W - 1, D_CONV), lambda i: (0, i, 0, 0))
    y, pool_new, conv_new = pl.pallas_call(
        functools.partial(_layer0_sample_kernel, layer=layer, seq=seq),
        grid=(n // tm,),
        in_specs=[tile, pool, conv] + specs,
        out_specs=[tile, pool, conv],
        out_shape=[jax.ShapeDtypeStruct(x.shape, F32),
                   jax.ShapeDtypeStruct(pool_t.shape, F32),
                   jax.ShapeDtypeStruct(state_conv.shape, F32)],
        scratch_shapes=[pltpu.VMEM((tb, HALO + seq, D_POOL), F32),
                        pltpu.VMEM((tb, HALO + seq, D_CONV), F32)],
        compiler_params=_params(1),
        name="layer0_sample",
    )(x, pool_t, state_conv, *arrs)
    return y, jnp.swapaxes(pool_new, 1, 2), conv_new


def _layer1(x, weights, tm, sub_rows, layer, seq, emit_v, name):
    n, d = x.shape
    arrs, specs = _split(weights)
    tile = pl.BlockSpec((tm, d), lambda i: (i, 0))
    out_specs = [tile, tile] if emit_v else [tile]
    out_shape = [jax.ShapeDtypeStruct(x.shape, F32)] * len(out_specs)
    return pl.pallas_call(
        functools.partial(_layer1_kernel, layer=layer, seq=seq, emit_v=emit_v,
                          sub_rows=sub_rows),
        grid=(n // tm,),
        in_specs=[tile] + specs,
        out_specs=out_specs,
        out_shape=out_shape,
        scratch_shapes=[pltpu.VMEM((SGU_HEADS, CHUNK, CHUNK), BF16),
                        pltpu.VMEM((CHUNK, D_SGU), F32),
                        pltpu.VMEM((tm, arrs[-2].shape[1]), BF16)],
        compiler_params=_params(1),
        name=name,
    )(x, *arrs)


def kernel(x_prompt, x_sample, state_pool, state_conv, g_mix_pre, g_mix_post, g_ffn_pre,
           g_ffn_post, w_in_ab, w_pool_grp, pool_scale, conv_w, w_out_ab, w_uv, g_v, w_spatial,
           b_spatial, w_out_c, w_up, w_down):
    bp, sp, d = x_prompt.shape
    bs, ss, _ = x_sample.shape
    assert ss < CHUNK and CHUNK % ss == 0 and CONV_W - 1 <= ss and sp % CHUNK == 0

    gains = ((g_mix_pre, None), (g_mix_post, None), (g_ffn_pre, None), (g_ffn_post, None))
    w0_small = gains + ((w_pool_grp, 0), (pool_scale, None), (conv_w, 0))

    (y0, pool_p, conv_p, w_in_b, w_eff_b, w_up0_b, w_down0_b,
     w_uv_b, w_out_c_b, w_up1_b, w_down1_b) = _layer0_prompt(
        x_prompt, w0_small, (w_in_ab, w_out_ab, w_up, w_down), 0,
        ((w_uv, 0), (w_out_c, 0), (w_up, 1), (w_down, 1)))
    w0 = gains + ((conv_w, 0), (w_in_b, None), (w_eff_b, None), (w_up0_b, None), (w_down0_b, None))
    w1 = gains + ((g_v, None), (w_spatial, 0), (b_spatial, 0),
                  (w_uv_b, None), (w_out_c_b, None), (w_up1_b, None), (w_down1_b, None))
    (y_prompt,) = _layer1(y0.reshape(bp * sp, d), w1, TILE_M_PROMPT_L1, SUB_TILE_ROWS_L1, 1, None,
                          False, "layer1_prompt")

    ys0, pool_s, conv_s = _layer0_sample(x_sample.reshape(bs * ss, d), state_pool, state_conv,
                                         w0, 0, ss)
    ys1, v_s = _layer1(ys0, w1, TILE_M_SAMPLE, SUB_TILE_ROWS, 1, ss, True, "layer1_sample")
    return (y_prompt.reshape(bp, sp, d), ys1.reshape(bs, ss, d), pool_p, pool_s, conv_p, conv_s,
            v_s.reshape(1, bs, ss, D_SGU))
```

```python
import functools

import jax
import jax.numpy as jnp
from jax import lax
from jax.experimental import pallas as pl
from jax.experimental.pallas import tpu as pltpu

D_MODEL = 1024
EPS = 1e-6
D_POOL = D_MODEL // 2
POOL_WINDOWS = (2, 4, 8, 16)
POOL_GC = D_POOL // len(POOL_WINDOWS)
POOL_PREV = max(POOL_WINDOWS) - 1
D_CONV = D_MODEL // 2
CONV_W = 3
D_SGU = D_MODEL
SGU_HEADS = 8
SGU_HD = D_SGU // SGU_HEADS
CHUNK = 128
PAST_LEN = 16384

HALO = 16
TILE_M_PROMPT = 512
TILE_M_SAMPLE = 512
SUB_TILE_ROWS = 256
SUB_GROUP = 2
TILE_M_PROMPT_L1 = 1024
SUB_TILE_ROWS_L1 = 512
STAGE_ROWS, STAGE_COLS = 256, 1024
STAGE_SLOTS = 5
VMEM_LIMIT_BYTES = 60 * 1024 * 1024
VMEM_LIMIT_BYTES_SMALL = 56 * 1024 * 1024

F32 = jnp.float32
BF16 = jnp.bfloat16


def _rms(x, g):
    return x * lax.rsqrt(jnp.mean(x * x, axis=-1, keepdims=True) + EPS) * g


def _dot(a, b):
    return jnp.dot(a, b, preferred_element_type=F32)


def _shift_rows(a, k):
    return pltpu.roll(a, k, axis=0)


def _window_sums(full_u):
    sums = []
    for g, w in enumerate(POOL_WINDOWS):
        s = full_u[:, g * POOL_GC:(g + 1) * POOL_GC]
        k = 1
        while k < w:
            s = s + _shift_rows(s, k)
            k *= 2
        sums.append(s)
    return sums


def _pool_diff(sums, u, pos):
    outs = []
    for g, w in enumerate(POOL_WINDOWS):
        inv_cnt = 1.0 / jnp.minimum(pos + 1.0, float(w))
        outs.append(sums[g] * inv_cnt - u[:, g * POOL_GC:(g + 1) * POOL_GC])
    return jnp.concatenate(outs, axis=1)


def _fold_pool_into_out(w_grp_ref, scale_ref, w_eff_ref):
    scale = scale_ref[...]
    for g in range(len(POOL_WINDOWS)):
        rows = slice(g * POOL_GC, (g + 1) * POOL_GC)
        a = w_grp_ref[g] * scale[:, rows]
        a_hi = a.astype(BF16)
        a_lo = (a - a_hi.astype(F32)).astype(BF16)
        b = w_eff_ref[rows, :]
        w_eff_ref[rows, :] = (_dot(a_hi, b) + _dot(a_lo, b)).astype(BF16)


def _conv_rows(full_cx, conv_w):
    return (conv_w[0:1] * _shift_rows(full_cx, 2) + conv_w[1:2] * _shift_rows(full_cx, 1)
            + conv_w[2:3] * full_cx)


def _gains(gain_refs, layer):
    return [r[layer:layer + 1, :] for r in gain_refs]


def _in_proj(x, g_pre, w_in_ref):
    p = _dot(_rms(x, g_pre).astype(BF16), w_in_ref[...])
    u = p[:, :D_POOL]
    cx = p[:, D_POOL + 2 * D_CONV:] * p[:, D_POOL:D_POOL + D_CONV]
    gate_b = p[:, D_POOL + D_CONV:D_POOL + 2 * D_CONV]
    return u, cx, gate_b


def _span_groups(tm, sub_rows=SUB_TILE_ROWS):
    assert tm % sub_rows == 0
    spans = [(r0, sub_rows) for r0 in range(0, tm, sub_rows)]
    return [spans[k:k + SUB_GROUP] for k in range(0, len(spans), SUB_GROUP)]


def _ffn_staged(xs, g_pre, g_post, w_up_ref, w_down_ref):
    acts = []
    for x in xs:
        a = jnp.maximum(_dot(_rms(x, g_pre).astype(BF16), w_up_ref[...]), 0.0)
        acts.append((a * a).astype(BF16))
    return [x + _rms(_dot(a, w_down_ref[...]), g_post) for x, a in zip(xs, acts)]


def _stream_cast(srcs, slab, dsts, stage, sem):
    ahead = STAGE_SLOTS - 1

    def make_plan(src, dst, first):
        rows, cols = dst.shape
        assert rows % STAGE_ROWS == 0 and cols % STAGE_COLS == 0
        ncol = cols // STAGE_COLS
        n = (rows // STAGE_ROWS) * ncol
        assert n >= ahead

        def window(k):
            r0 = pl.multiple_of((k // ncol) * STAGE_ROWS, STAGE_ROWS)
            c0 = pl.multiple_of((k % ncol) * STAGE_COLS, STAGE_COLS)
            return pl.ds(r0, STAGE_ROWS), pl.ds(c0, STAGE_COLS)

        def fetch(k):
            rs, cs = window(k)
            slot = (first + k) % STAGE_SLOTS
            return pltpu.make_async_copy(src.at[slab, rs, cs], stage.at[slot], sem.at[slot])

        return n, window, fetch

    plans, first = [], 0
    for src, dst in zip(srcs, dsts):
        plans.append(make_plan(src, dst, first) + (first,))
        first += plans[-1][0]

    for k in range(ahead):
        plans[0][2](k).start()
    for j, ((n, window, fetch, first), dst) in enumerate(zip(plans, dsts)):
        next_fetch = plans[j + 1][2] if j + 1 < len(plans) else None

        def body(k, carry, n=n, window=window, fetch=fetch, first=first, dst=dst,
                 next_fetch=next_fetch):
            @pl.when(k + ahead < n)
            def _():
                fetch(k + ahead).start()

            if next_fetch is not None:
                @pl.when(k + ahead >= n)
                def _():
                    next_fetch(k + ahead - n).start()

            fetch(k).wait()
            rs, cs = window(k)
            dst[rs, cs] = stage[(first + k) % STAGE_SLOTS].astype(BF16)
            return carry

        lax.fori_loop(0, n, body, 0)


def _export_copies(bufs, outs, sem):
    return [pltpu.make_async_copy(buf, out, sem.at[j]) for j, (buf, out) in enumerate(zip(bufs, outs))]


def _layer0_prompt_kernel(x_ref, g0, g1, g2, g3, w_grp_ref, scale_ref, conv_w_ref,
                          *rest, layer, n_cast):
    w_hbm, cast_in, rest = rest[:4], rest[4:4 + n_cast], rest[4 + n_cast:]
    (y_ref, pool_out_ref, conv_out_ref), w_exp, rest = rest[:3], rest[3:7], rest[7:]
    cast_out, rest = rest[:n_cast], rest[n_cast:]
    halo_u, halo_cx, w_in_ref, w_eff_scr, w_up_ref, w_down_ref, stage, sem_in, sem_out = rest
    w_bufs = (w_in_ref, w_eff_scr, w_up_ref, w_down_ref)
    tm = x_ref.shape[0]
    t = pl.program_id(1)
    first = (pl.program_id(0) == 0) & (t == 0)
    last = (pl.program_id(0) == pl.num_programs(0) - 1) & (t == pl.num_programs(1) - 1)
    g_pre, g_post, gf_pre, gf_post = _gains((g0, g1, g2, g3), layer)

    for src, dst in zip(cast_in, cast_out):
        dst[...] = src[...].astype(BF16)

    @pl.when(first)
    def _():
        _stream_cast(w_hbm, layer // 2, w_bufs, stage, sem_in)
        _fold_pool_into_out(w_grp_ref, scale_ref, w_eff_scr)
        for c in _export_copies(w_bufs, w_exp, sem_out):
            c.start()

    @pl.when(last)
    def _():
        for c in _export_copies(w_bufs, w_exp, sem_out):
            c.wait()

    @pl.when(t == 0)
    def _():
        halo_u[...] = jnp.zeros_like(halo_u)
        halo_cx[...] = jnp.zeros_like(halo_cx)

    hist_u, hist_cx = halo_u[...], halo_cx[...]
    for spans in _span_groups(tm):
        xs = [x_ref[pl.ds(r0, ts), :] for r0, ts in spans]
        proj = [_in_proj(x, g_pre, w_in_ref) for x in xs]

        yabs = []
        for (r0, ts), (u, cx, gate_b) in zip(spans, proj):
            full_u = jnp.concatenate([hist_u, u], axis=0)
            full_cx = jnp.concatenate([hist_cx, cx], axis=0)
            hist_u, hist_cx = u[ts - HALO:], cx[ts - HALO:]
            pos = (lax.broadcasted_iota(jnp.int32, (ts, 1), 0) + (t * tm + r0)).astype(F32)
            sums = [s[HALO:] for s in _window_sums(full_u)]
            d = _pool_diff(sums, u, pos)
            cz = _conv_rows(full_cx, conv_w_ref[...])[HALO:]
            yabs.append(jnp.concatenate([d, gate_b * cz], axis=1).astype(BF16))

        x1 = [x + _rms(_dot(yab, w_eff_scr[...]), g_post) for x, yab in zip(xs, yabs)]
        for (r0, ts), y in zip(spans, _ffn_staged(x1, gf_pre, gf_post, w_up_ref, w_down_ref)):
            y_ref[pl.ds(r0, ts), :] = y
    halo_u[...] = hist_u
    halo_cx[...] = hist_cx

    pool_out_ref[...] = halo_u[HALO - POOL_PREV:, :]
    conv_out_ref[...] = halo_cx[HALO - (CONV_W - 1):, :]


def _layer0_sample_kernel(x_ref, pool_ref, conv_ref, g0, g1, g2, g3, conv_w_ref,
                          w_in_ref, w_eff_ref, w_up_ref, w_down_ref,
                          y_ref, pool_out_ref, conv_out_ref, full_u_scr, full_cx_scr,
                          *, layer, seq):
    tm = x_ref.shape[0]
    grp = HALO + seq
    g_pre, g_post, gf_pre, gf_post = _gains((g0, g1, g2, g3), layer)

    (spans,) = _span_groups(tm, tm // SUB_GROUP)
    ts = spans[0][1]
    tbs = ts // seq
    xs = [x_ref[pl.ds(r0, ts), :] for r0, ts in spans]
    proj = [_in_proj(x, g_pre, w_in_ref) for x in xs]

    def new_rows(a):
        return a.reshape(tbs, grp, a.shape[-1])[:, HALO:].reshape(ts, a.shape[-1])

    pos = (lax.broadcasted_iota(jnp.int32, (tbs, seq, 1), 1).reshape(ts, 1) + PAST_LEN).astype(F32)
    yabs = []
    for (r0, _), (u, cx, gate_b) in zip(spans, proj):
        sq = pl.ds(r0 // seq, tbs)
        full_u_scr[sq, :HALO - POOL_PREV, :] = jnp.zeros((tbs, HALO - POOL_PREV, D_POOL), F32)
        for r in range(POOL_PREV):
            full_u_scr[sq, HALO - POOL_PREV + r, :] = pool_ref[r, sq, :]
        full_u_scr[sq, HALO:, :] = u.reshape(tbs, seq, D_POOL)
        full_cx_scr[sq, :HALO - (CONV_W - 1), :] = jnp.zeros((tbs, HALO - (CONV_W - 1), D_CONV), F32)
        full_cx_scr[sq, HALO - (CONV_W - 1):HALO, :] = conv_ref[sq]
        full_cx_scr[sq, HALO:, :] = cx.reshape(tbs, seq, D_CONV)
        for r in range(POOL_PREV):
            pool_out_ref[r, sq, :] = full_u_scr[sq, grp - POOL_PREV + r, :]
        conv_out_ref[sq] = full_cx_scr[sq, grp - (CONV_W - 1):, :]
        full_u = full_u_scr[sq].reshape(tbs * grp, D_POOL)
        full_cx = full_cx_scr[sq].reshape(tbs * grp, D_CONV)
        sums = [new_rows(s) for s in _window_sums(full_u)]
        d = _pool_diff(sums, u, pos)
        cz = new_rows(_conv_rows(full_cx, conv_w_ref[...]))
        yabs.append(jnp.concatenate([d, gate_b * cz], axis=1).astype(BF16))

    x1 = [x + _rms(_dot(yab, w_eff_ref[...]), g_post) for x, yab in zip(xs, yabs)]
    for (r0, _), y in zip(spans, _ffn_staged(x1, gf_pre, gf_post, w_up_ref, w_down_ref)):
        y_ref[pl.ds(r0, ts), :] = y


def _sgu_tables(w_sp_ref, b_sp_ref, wm_scr, bias_scr, seq):
    r = lax.broadcasted_iota(jnp.int32, (CHUNK, CHUNK), 0)
    c = lax.broadcasted_iota(jnp.int32, (CHUNK, CHUNK), 1)
    keep = r >= c
    b = b_sp_ref[...]
    if seq is not None:
        keep = keep & (r // seq == c // seq)
        pick = (r % seq == c).astype(BF16)
        pick_t = (c % seq == r).astype(BF16)
        b = jnp.where(lax.broadcasted_iota(jnp.int32, b.shape, 1) < seq, b, 0.0)
        k = seq
        while k < CHUNK:
            b = b + pltpu.roll(b, k, axis=1)
            k *= 2
    mask = keep.astype(F32)
    for hh in range(SGU_HEADS):
        w = w_sp_ref[hh]
        if seq is not None:
            w = _dot(_dot(pick, w.astype(BF16)).astype(BF16), pick_t)
        wm_scr[hh] = (w * mask).astype(BF16)
        rows = jnp.broadcast_to(b[hh:hh + 1, :], (CHUNK, SGU_HD))
        bias_scr[:, hh * SGU_HD:(hh + 1) * SGU_HD] = rows.T


def _layer1_kernel(x_ref, g0, g1, g2, g3, g_v_ref, w_sp_ref, b_sp_ref,
                   w_uv_ref, w_out_ref, w_up_ref, w_down_ref, y_ref, *rest, layer, seq, emit_v,
                   sub_rows):
    if emit_v:
        v_ref, wm_scr, bias_scr = rest
    else:
        wm_scr, bias_scr = rest
    tm = x_ref.shape[0]
    g_pre, g_post, gf_pre, gf_post = _gains((g0, g1, g2, g3), layer)

    @pl.when(pl.program_id(0) == 0)
    def _():
        _sgu_tables(w_sp_ref, b_sp_ref, wm_scr, bias_scr, seq)

    bias = bias_scr[...]
    for spans in _span_groups(tm, sub_rows):
        xs = [x_ref[pl.ds(r0, ts), :] for r0, ts in spans]
        ps = [_dot(_rms(x, g_pre).astype(BF16), w_uv_ref[...]) for x in xs]

        gated = []
        for (r0, ts), p in zip(spans, ps):
            nc = ts // CHUNK
            u = p[:, :D_SGU]
            v = _rms(p[:, D_SGU:], g_v_ref[...])
            if emit_v:
                v_ref[pl.ds(r0, ts), :] = v
            vb = v.astype(BF16)
            cols = []
            for hh in range(SGU_HEADS):
                lanes = slice(hh * SGU_HD, (hh + 1) * SGU_HD)
                rhs = jnp.concatenate(
                    [vb[ci * CHUNK:(ci + 1) * CHUNK, lanes] for ci in range(nc)], axis=1)
                out = _dot(wm_scr[hh], rhs)
                cols.append(jnp.concatenate(
                    [out[:, ci * SGU_HD:(ci + 1) * SGU_HD] + bias[:, lanes] for ci in range(nc)],
                    axis=0))
            gated.append((u * jnp.concatenate(cols, axis=1)).astype(BF16))

        x1 = [x + _rms(_dot(a, w_out_ref[...]), g_post) for x, a in zip(xs, gated)]
        for (r0, ts), y in zip(spans, _ffn_staged(x1, gf_pre, gf_post, w_up_ref, w_down_ref)):
            y_ref[pl.ds(r0, ts), :] = y


def _resident(arr, layer=None):
    if layer is None:
        nd = arr.ndim
        return pl.BlockSpec(arr.shape, lambda *_: (0,) * nd, pipeline_mode=pl.Buffered(1))
    nd = arr.ndim - 1
    return pl.BlockSpec((None,) + arr.shape[1:], lambda *_: (layer,) + (0,) * nd,
                        pipeline_mode=pl.Buffered(1))


def _params(n_axes, vmem_limit_bytes=VMEM_LIMIT_BYTES):
    return pltpu.CompilerParams(dimension_semantics=("arbitrary",) * n_axes,
                                vmem_limit_bytes=vmem_limit_bytes)


def _split(weights):
    return [a for a, _ in weights], [_resident(a, l) for a, l in weights]


def _layer0_prompt(x, weights, big, layer, to_cast):
    b, s, d = x.shape
    tm = TILE_M_PROMPT
    nt = s // tm
    steps = b * nt
    arrs, specs = _split(weights)
    any_spec = pl.BlockSpec(memory_space=pl.ANY)
    big_shapes = [w.shape[1:] for w in big]
    tile = pl.BlockSpec((None, tm, d), lambda i, j: (i, j, 0))
    pool = pl.BlockSpec((None, None, POOL_PREV, D_POOL), lambda i, j: (0, i, 0, 0))
    conv = pl.BlockSpec((None, None, CONV_W - 1, D_CONV), lambda i, j: (0, i, 0, 0))
    cast_in, cast_out, cast_shapes = [], [], []
    for w, slab in to_cast:
        _, rows, cols = w.shape
        assert rows % steps == 0
        cast_in.append(pl.BlockSpec((None, rows // steps, cols),
                                    lambda i, j, slab=slab: (slab, i * nt + j, 0)))
        cast_out.append(pl.BlockSpec((rows // steps, cols), lambda i, j: (i * nt + j, 0)))
        cast_shapes.append(jax.ShapeDtypeStruct((rows, cols), BF16))
    return pl.pallas_call(
        functools.partial(_layer0_prompt_kernel, layer=layer, n_cast=len(to_cast)),
        grid=(b, nt),
        in_specs=[tile] + specs + [any_spec] * len(big) + cast_in,
        out_specs=[tile, pool, conv] + [any_spec] * len(big) + cast_out,
        out_shape=[jax.ShapeDtypeStruct(x.shape, F32),
                   jax.ShapeDtypeStruct((1, b, POOL_PREV, D_POOL), F32),
                   jax.ShapeDtypeStruct((1, b, CONV_W - 1, D_CONV), F32)]
        + [jax.ShapeDtypeStruct(sh, BF16) for sh in big_shapes] + cast_shapes,
        scratch_shapes=[pltpu.VMEM((HALO, D_POOL), F32), pltpu.VMEM((HALO, D_CONV), F32)]
        + [pltpu.VMEM(sh, BF16) for sh in big_shapes]
        + [pltpu.VMEM((STAGE_SLOTS, STAGE_ROWS, STAGE_COLS), F32),
           pltpu.SemaphoreType.DMA((STAGE_SLOTS,)),
           pltpu.SemaphoreType.DMA((len(big),))],
        compiler_params=_params(2, VMEM_LIMIT_BYTES_SMALL),
        name="layer0_prompt",
    )(x, *arrs, *big, *[w for w, _ in to_cast])


def _layer0_sample(x, state_pool, state_conv, weights, layer, seq):
    n, d = x.shape
    tm = TILE_M_SAMPLE
    tb = tm // seq
    arrs, specs = _split(weights)
    tile = pl.BlockSpec((tm, d), lambda i: (i, 0))
    pool_t = jnp.swapaxes(state_pool, 1, 2)
    pool = pl.BlockSpec((None, POOL_PREV, tb, D_POOL), lambda i: (0, 0, i, 0))
    conv = pl.BlockSpec((None, tb, CONV_W - 1, D_CONV), lambda i: (0, i, 0, 0))
    y, pool_new, conv_new = pl.pallas_call(
        functools.partial(_layer0_sample_kernel, layer=layer, seq=seq),
        grid=(n // tm,),
        in_specs=[tile, pool, conv] + specs,
        out_specs=[tile, pool, conv],
        out_shape=[jax.ShapeDtypeStruct(x.shape, F32),
                   jax.ShapeDtypeStruct(pool_t.shape, F32),
                   jax.ShapeDtypeStruct(state_conv.shape, F32)],
        scratch_shapes=[pltpu.VMEM((tb, HALO + seq, D_POOL), F32),
                        pltpu.VMEM((tb, HALO + seq, D_CONV), F32)],
        compiler_params=_params(1),
        name="layer0_sample",
    )(x, pool_t, state_conv, *arrs)
    return y, jnp.swapaxes(pool_new, 1, 2), conv_new


def _layer1(x, weights, tm, sub_rows, layer, seq, emit_v, name):
    n, d = x.shape
    arrs, specs = _split(weights)
    tile = pl.BlockSpec((tm, d), lambda i: (i, 0))
    out_specs = [tile, tile] if emit_v else [tile]
    out_shape = [jax.ShapeDtypeStruct(x.shape, F32)] * len(out_specs)
    return pl.pallas_call(
        functools.partial(_layer1_kernel, layer=layer, seq=seq, emit_v=emit_v,
                          sub_rows=sub_rows),
        grid=(n // tm,),
        in_specs=[tile] + specs,
        out_specs=out_specs,
        out_shape=out_shape,
        scratch_shapes=[pltpu.VMEM((SGU_HEADS, CHUNK, CHUNK), BF16),
                        pltpu.VMEM((CHUNK, D_SGU), F32)],
        compiler_params=_params(
            1, VMEM_LIMIT_BYTES if sub_rows > SUB_TILE_ROWS else VMEM_LIMIT_BYTES_SMALL),
        name=name,
    )(x, *arrs)


def kernel(x_prompt, x_sample, state_pool, state_conv, g_mix_pre, g_mix_post, g_ffn_pre,
           g_ffn_post, w_in_ab, w_pool_grp, pool_scale, conv_w, w_out_ab, w_uv, g_v, w_spatial,
           b_spatial, w_out_c, w_up, w_down):
    bp, sp, d = x_prompt.shape
    bs, ss, _ = x_sample.shape
    assert ss < CHUNK and CHUNK % ss == 0 and CONV_W - 1 <= ss and sp % CHUNK == 0

    gains = ((g_mix_pre, None), (g_mix_post, None), (g_ffn_pre, None), (g_ffn_post, None))
    w0_small = gains + ((w_pool_grp, 0), (pool_scale, None), (conv_w, 0))

    (y0, pool_p, conv_p, w_in_b, w_eff_b, w_up0_b, w_down0_b,
     w_uv_b, w_out_c_b, w_up1_b, w_down1_b) = _layer0_prompt(
        x_prompt, w0_small, (w_in_ab, w_out_ab, w_up, w_down), 0,
        ((w_uv, 0), (w_out_c, 0), (w_up, 1), (w_down, 1)))
    w0 = gains + ((conv_w, 0), (w_in_b, None), (w_eff_b, None), (w_up0_b, None), (w_down0_b, None))
    w1 = gains + ((g_v, None), (w_spatial, 0), (b_spatial, 0),
                  (w_uv_b, None), (w_out_c_b, None), (w_up1_b, None), (w_down1_b, None))
    (y_prompt,) = _layer1(y0.reshape(bp * sp, d), w1, TILE_M_PROMPT_L1, SUB_TILE_ROWS_L1, 1, None,
                          False, "layer1_prompt")

    ys0, pool_s, conv_s = _layer0_sample(x_sample.reshape(bs * ss, d), state_pool, state_conv,
                                         w0, 0, ss)
    ys1, v_s = _layer1(ys0, w1, TILE_M_SAMPLE, SUB_TILE_ROWS, 1, ss, True, "layer1_sample")
    return (y_prompt.reshape(bp, sp, d), ys1.reshape(bs, ss, d), pool_p, pool_s, conv_p, conv_s,
            v_s.reshape(1, bs, ss, D_SGU))
```
